```python
import jax, jax.numpy as jnp
from jax import lax
import numpy as np

D_MODEL = 1024
BATCH = 4
SEQ = 8192
DEPTH = 1
DEC_BATCH = 16
DEC_SEQ = 32
PAST_LEN = 4096

CHUNK = 64
HEAD_DIM = 64
N_HEADS_A = 8
N_HEADS_B = 8
IDX_HEADS = 8
IDX_DIM = 64
TOPK_MAX = 256
D_FF = 4 * D_MODEL
ROPE_THETA = 10000.0
EPS = 1e-6
SB_Q_BLOCK = 128
DSA_Q_BLOCK = 64
WIDTH_A = N_HEADS_A * HEAD_DIM
WIDTH_B = N_HEADS_B * HEAD_DIM
SPLIT_SIZES = (WIDTH_A, WIDTH_A, WIDTH_A, IDX_HEADS * IDX_DIM, IDX_DIM, IDX_HEADS,
               WIDTH_B, WIDTH_B, WIDTH_B, D_MODEL, D_MODEL)
SPLIT_OFFSETS = tuple(int(v) for v in np.cumsum(SPLIT_SIZES)[:-1])
D_IN = int(sum(SPLIT_SIZES))

kernel_name = "hybrid_dsa_stickbreaking_stream_step"


def rmsnorm(x, g):
    xf = x.astype(jnp.float32)
    y = xf * lax.rsqrt(jnp.mean(xf * xf, axis=-1, keepdims=True) + EPS)
    return (y * g.astype(jnp.float32)).astype(x.dtype)


def rope(x, pos):
    half = x.shape[-1] // 2
    inv_freq = jnp.power(ROPE_THETA, -jnp.arange(half, dtype=jnp.float32) / half)
    ang = pos.astype(jnp.float32)[:, None] * inv_freq[None, :]
    cos = jnp.cos(ang)[None, :, None, :]
    sin = jnp.sin(ang)[None, :, None, :]
    xf = x.astype(jnp.float32)
    x1, x2 = xf[..., :half], xf[..., half:]
    return jnp.concatenate([x1 * cos - x2 * sin, x2 * cos + x1 * sin], axis=-1).astype(x.dtype)


def project_inputs(x, pos, g_mix, w_in, g_qn, g_kn):
    B, L, _ = x.shape
    h = rmsnorm(x, g_mix)
    z = jnp.einsum('bld,de->ble', h, w_in)
    q_a, k_a, v_a, q_i, k_i, w_i, q_b, k_b, v_b, gate_a, gate_b = jnp.split(z, SPLIT_OFFSETS, axis=-1)
    q_a = rope(rmsnorm(q_a.reshape(B, L, N_HEADS_A, HEAD_DIM), g_qn), pos)
    k_a = rope(rmsnorm(k_a.reshape(B, L, N_HEADS_A, HEAD_DIM), g_kn), pos)
    v_a = v_a.reshape(B, L, N_HEADS_A, HEAD_DIM)
    q_i = rope(q_i.reshape(B, L, IDX_HEADS, IDX_DIM), pos)
    k_i = rope(k_i[:, :, None, :], pos)[:, :, 0, :]
    w_i = w_i * ((IDX_HEADS ** -0.5) * (IDX_DIM ** -0.5))
    q_b = q_b.reshape(B, L, N_HEADS_B, HEAD_DIM)
    k_b = k_b.reshape(B, L, N_HEADS_B, HEAD_DIM)
    v_b = v_b.reshape(B, L, N_HEADS_B, HEAD_DIM)
    return q_a, k_a, v_a, q_i, k_i, w_i, q_b, k_b, v_b, gate_a, gate_b


def gather_rows(a, idx):
    return jax.vmap(lambda ab, ib: ab[ib])(a, idx)


def dsa_block(q, q_i, w_i, qpos, k, v, k_i, topk):
    L = k.shape[1]
    kpos = jnp.arange(L, dtype=jnp.int32)
    s_idx = jnp.einsum('bqhd,bld->bqhl', q_i, k_i).astype(jnp.float32)
    score = jnp.einsum('bqhl,bqh->bql', jax.nn.relu(s_idx), w_i.astype(jnp.float32))
    admissible = (kpos // CHUNK)[None, :] <= (qpos // CHUNK)[:, None]
    score = jnp.where(admissible[None], score, -jnp.inf)
    _, idx = lax.top_k(score, topk)
    valid = (idx // CHUNK) <= (qpos // CHUNK)[None, :, None]
    kg = gather_rows(k, idx)
    vg = gather_rows(v, idx)
    logits = jnp.einsum('bqhd,bqkhd->bqhk', q, kg).astype(jnp.float32) * (HEAD_DIM ** -0.5)
    logits = jnp.where(valid[:, :, None, :], logits, -jnp.inf)
    p = jax.nn.softmax(logits, axis=-1)
    return jnp.einsum('bqhk,bqkhd->bqhd', p.astype(v.dtype), vg)


def stick_breaking_block(q, qpos, k, v):
    L = k.shape[1]
    kpos = jnp.arange(L, dtype=jnp.int32)
    z = jnp.einsum('bqhd,blhd->bhql', q, k).astype(jnp.float32) * (HEAD_DIM ** -0.5)
    causal = (kpos[None, :] < qpos[:, None])[None, None]
    sp = jnp.where(causal, jax.nn.softplus(z), 0.0)
    tail = lax.cumsum(sp, axis=3, reverse=True) - sp
    a = jnp.where(causal, jnp.exp(jax.nn.log_sigmoid(z) - tail), 0.0)
    return jnp.einsum('bhql,blhd->bqhd', a.astype(v.dtype), v)


def to_blocks(a, size):
    B, S = a.shape[:2]
    return jnp.moveaxis(a.reshape((B, S // size, size) + a.shape[2:]), 1, 0)


def from_blocks(a):
    nb, B, size = a.shape[:3]
    return jnp.moveaxis(a, 0, 1).reshape((B, nb * size) + a.shape[3:])


def merge_and_ffn(x, o_a, o_b, gate_a, gate_b, w_branch_a, w_branch_b, w_out, g_ffn, w_up, w_down):
    B, L = x.shape[:2]
    pa = jnp.einsum('blc,cd->bld', o_a.reshape(B, L, WIDTH_A), w_branch_a)
    pb = jnp.einsum('blc,cd->bld', o_b.reshape(B, L, WIDTH_B), w_branch_b)
    m = jax.nn.sigmoid(gate_a) * pa + jax.nn.sigmoid(gate_b) * pb
    h = x + jnp.einsum('bld,de->ble', m, w_out)
    u = jnp.einsum('bld,df->blf', rmsnorm(h, g_ffn), w_up)
    return h + jnp.einsum('blf,fd->bld', jnp.square(jax.nn.relu(u)), w_down)


def setup_inputs(seed: int = 0) -> dict:
    key = jax.random.key(seed)
    ks = jax.random.split(key, 18)
    nrm = jax.random.normal
    f32 = jnp.float32
    return {
        "x_prompt": nrm(ks[0], (BATCH, SEQ, D_MODEL), f32),
        "x_sample": nrm(ks[1], (DEC_BATCH, DEC_SEQ, D_MODEL), f32),
        "cache_k_a": nrm(ks[2], (DEC_BATCH, PAST_LEN, N_HEADS_A, HEAD_DIM), f32),
        "cache_v_a": nrm(ks[3], (DEC_BATCH, PAST_LEN, N_HEADS_A, HEAD_DIM), f32),
        "cache_k_idx": nrm(ks[4], (DEC_BATCH, PAST_LEN, IDX_DIM), f32),
        "cache_k_sb": nrm(ks[5], (DEC_BATCH, PAST_LEN, N_HEADS_B, HEAD_DIM), f32),
        "cache_v_sb": nrm(ks[6], (DEC_BATCH, PAST_LEN, N_HEADS_B, HEAD_DIM), f32),
        "g_mix": 1.0 + 0.01 * nrm(ks[7], (D_MODEL,), f32),
        "w_in": nrm(ks[8], (D_MODEL, D_IN), f32) * D_MODEL ** -0.5,
        "g_qn": 1.0 + 0.01 * nrm(ks[9], (HEAD_DIM,), f32),
        "g_kn": 1.0 + 0.01 * nrm(ks[10], (HEAD_DIM,), f32),
        "w_branch_a": nrm(ks[11], (WIDTH_A, D_MODEL), f32) * WIDTH_A ** -0.5,
        "w_branch_b": nrm(ks[12], (WIDTH_B, D_MODEL), f32) * WIDTH_B ** -0.5,
        "w_out": nrm(ks[13], (D_MODEL, D_MODEL), f32) * D_MODEL ** -0.5,
        "g_ffn": 1.0 + 0.01 * nrm(ks[14], (D_MODEL,), f32),
        "w_up": nrm(ks[15], (D_MODEL, D_FF), f32) * D_MODEL ** -0.5,
        "w_down": nrm(ks[16], (D_FF, D_MODEL), f32) * D_FF ** -0.5,
    }


def reference(x_prompt, x_sample, cache_k_a, cache_v_a, cache_k_idx, cache_k_sb, cache_v_sb,
              g_mix, w_in, g_qn, g_kn, w_branch_a, w_branch_b, w_out, g_ffn, w_up, w_down):
    S = x_prompt.shape[1]
    T = x_sample.shape[1]
    past = cache_k_a.shape[1]

    pos_p = jnp.arange(S, dtype=jnp.int32)
    xp = x_prompt
    for _ in range(DEPTH):
        q_a, k_a_p, v_a_p, q_i, k_idx_p, w_i, q_b, k_sb_p, v_sb_p, ga, gb = project_inputs(
            xp, pos_p, g_mix, w_in, g_qn, g_kn)
        topk_p = min(TOPK_MAX, S // 4)
        o_a = from_blocks(lax.map(
            lambda blk: dsa_block(blk[0], blk[1], blk[2], blk[3], k_a_p, v_a_p, k_idx_p, topk_p),
            (to_blocks(q_a, DSA_Q_BLOCK), to_blocks(q_i, DSA_Q_BLOCK), to_blocks(w_i, DSA_Q_BLOCK),
             pos_p.reshape(-1, DSA_Q_BLOCK))))
        o_b = from_blocks(lax.map(
            lambda blk: stick_breaking_block(blk[0], blk[1], k_sb_p, v_sb_p),
            (to_blocks(q_b, SB_Q_BLOCK), pos_p.reshape(-1, SB_Q_BLOCK))))
        xp = merge_and_ffn(xp, o_a, o_b, ga, gb, w_branch_a, w_branch_b, w_out, g_ffn, w_up, w_down)
    y_prompt = xp

    pos_s = past + jnp.arange(T, dtype=jnp.int32)
    xs = x_sample
    for _ in range(DEPTH):
        q_a_s, k_a_s, v_a_s, q_i_s, k_idx_s, w_i_s, q_b_s, k_sb_s, v_sb_s, ga_s, gb_s = project_inputs(
            xs, pos_s, g_mix, w_in, g_qn, g_kn)
        ka_all = jnp.concatenate([cache_k_a, k_a_s], axis=1)
        va_all = jnp.concatenate([cache_v_a, v_a_s], axis=1)
        ki_all = jnp.concatenate([cache_k_idx, k_idx_s], axis=1)
        kb_all = jnp.concatenate([cache_k_sb, k_sb_s], axis=1)
        vb_all = jnp.concatenate([cache_v_sb, v_sb_s], axis=1)
        topk_s = min(TOPK_MAX, (past + T) // 4)
        o_a_s = dsa_block(q_a_s, q_i_s, w_i_s, pos_s, ka_all, va_all, ki_all, topk_s)
        o_b_s = stick_breaking_block(q_b_s, pos_s, kb_all, vb_all)
        xs = merge_and_ffn(xs, o_a_s, o_b_s, ga_s, gb_s, w_branch_a, w_branch_b, w_out, g_ffn, w_up, w_down)
    y_sample = xs

    return (y_prompt, y_sample, k_a_p, v_a_p, k_idx_p, k_sb_p, v_sb_p,
            k_a_s, v_a_s, k_idx_s, k_sb_s, v_sb_s)
```

```python
import functools

import numpy as np
import jax
import jax.numpy as jnp
from jax import lax
from jax.experimental import pallas as pl
from jax.experimental.pallas import tpu as pltpu

D_MODEL = 1024
HEAD_DIM = 64
N_HEADS = 8
WIDTH = N_HEADS * HEAD_DIM
IDX_DIM = 64
CHUNK = 64
TOPK_MAX = 256
D_FF = 4 * D_MODEL
ROPE_THETA = 10000.0
EPS = 1e-6
SPLIT_SIZES = (WIDTH, WIDTH, WIDTH, N_HEADS * IDX_DIM, IDX_DIM, N_HEADS,
               WIDTH, WIDTH, WIDTH, D_MODEL, D_MODEL)

LANES = 128
PAIR = 2 * HEAD_DIM
N_PAIRS = N_HEADS // 2
QK_SCALE = HEAD_DIM ** -0.5
W_IDX_SCALE = (N_HEADS ** -0.5) * (IDX_DIM ** -0.5)
INT_MIN = -2 ** 31
NEG_BIG = -1e30
SB_EXIT_TAIL = 105.0
VMEM_LIMIT = 56 * 1024 * 1024

F32 = jnp.float32
BF16 = jnp.bfloat16
_NT = (((1,), (1,)), ((), ()))


def _resident(shape, index_map):
    return pl.BlockSpec(shape, index_map, pipeline_mode=pl.Buffered(1))


def _rope(x, cs, sn):
    w = x.shape[1]
    lane = lax.broadcasted_iota(jnp.int32, x.shape, 1)
    first_half = (lane & (HEAD_DIM // 2)) == 0
    partner = jnp.where(first_half, pltpu.roll(x, w - HEAD_DIM // 2, 1),
                        pltpu.roll(x, HEAD_DIM // 2, 1))
    reps = w // LANES
    if reps > 1:
        cs = jnp.concatenate([cs] * reps, axis=1)
        sn = jnp.concatenate([sn] * reps, axis=1)
    return x * cs + partner * sn


def _head_rmsnorm(z, grp, g):
    zz = z * z
    hi = zz.astype(BF16)
    lo = (zz - hi.astype(F32)).astype(BF16)
    ss = (jnp.dot(hi, grp, preferred_element_type=F32)
          + jnp.dot(lo, grp, preferred_element_type=F32))
    return z * lax.rsqrt(ss * (1.0 / HEAD_DIM) + EPS) * g


def _proj_kernel(x_ref, gmix_ref, w_ref, gqn_ref, gkn_ref, cs_ref, sn_ref, grp_ref,
                 ka_ref, va_ref, ki_ref, kb_ref, vb_ref,
                 qa16_ref, ka16_ref, va16_ref, qi16_ref, ki16_ref,
                 qb16_ref, kb16_ref, vb16_ref, wi_ref):
    x = x_ref[...]
    h = (x * lax.rsqrt(jnp.mean(x * x, axis=-1, keepdims=True) + EPS)
         * gmix_ref[...]).astype(BF16)
    cs = cs_ref[...]
    sn = sn_ref[...]
    grp = grp_ref[...]

    def seg(k, width=WIDTH):
        return jnp.dot(h, w_ref[:, k * WIDTH:k * WIDTH + width],
                       preferred_element_type=F32)

    qa = _rope(_head_rmsnorm(seg(0), grp, gqn_ref[...]), cs, sn)
    qa16_ref[...] = (qa * QK_SCALE).astype(BF16)
    ka = _rope(_head_rmsnorm(seg(1), grp, gkn_ref[...]), cs, sn)
    ka_ref[...] = ka
    ka16_ref[...] = ka.astype(BF16)
    va = seg(2)
    va_ref[...] = va
    va16_ref[...] = va.astype(BF16)
    qi16_ref[...] = _rope(seg(3), cs, sn).astype(BF16)
    qb16_ref[...] = (seg(4) * QK_SCALE).astype(BF16)
    kb = seg(5)
    kb_ref[...] = kb
    kb16_ref[...] = kb.astype(BF16)
    vb = seg(6)
    vb_ref[...] = vb
    vb16_ref[...] = vb.astype(BF16)
    misc = seg(7, LANES)
    ki = _rope(misc, cs, sn)[:, :IDX_DIM]
    ki_ref[...] = ki
    ki16 = ki.astype(BF16)
    ki16_ref[...] = jnp.concatenate([ki16, ki16], axis=1)
    wi_ref[...] = misc * W_IDX_SCALE


def _project(x2d, tabs, wts, tm):
    n = x2d.shape[0]
    cs, sn = tabs
    nt = cs.shape[0] // tm
    row = lambda i: (i, 0)
    const = lambda i: (0, 0)
    tab = lambda i: (i % nt, 0)
    f32_leaf = lambda w: jax.ShapeDtypeStruct((n, w), F32)
    b16 = lambda w: jax.ShapeDtypeStruct((n, w), BF16)
    out_shape = [f32_leaf(WIDTH), f32_leaf(WIDTH), f32_leaf(IDX_DIM), f32_leaf(WIDTH), f32_leaf(WIDTH),
                 b16(WIDTH), b16(WIDTH), b16(WIDTH), b16(WIDTH), b16(LANES),
                 b16(WIDTH), b16(WIDTH), b16(WIDTH), f32_leaf(LANES)]
    out_specs = [pl.BlockSpec((tm, s.shape[1]), row) for s in out_shape]
    w_proj = wts["w_proj"]
    return pl.pallas_call(
        _proj_kernel,
        grid=(n // tm,),
        in_specs=[pl.BlockSpec((tm, D_MODEL), row),
                  pl.BlockSpec((1, D_MODEL), const),
                  _resident(w_proj.shape, const),
                  pl.BlockSpec((1, WIDTH), const),
                  pl.BlockSpec((1, WIDTH), const),
                  pl.BlockSpec((tm, LANES), tab),
                  pl.BlockSpec((tm, LANES), tab),
                  _resident((WIDTH, WIDTH), const)],
        out_specs=out_specs,
        out_shape=out_shape,
        compiler_params=pltpu.CompilerParams(dimension_semantics=("arbitrary",),
                                             vmem_limit_bytes=VMEM_LIMIT),
        name="proj",
    )(x2d, wts["g_mix"], w_proj, wts["g_qn"], wts["g_kn"], cs, sn, wts["grp"])


def _sort_key(x):
    b = lax.bitcast_convert_type(x, jnp.int32)
    return b ^ ((b >> 31) & 0x7FFFFFFF)


def _mask_heads(src_ref, dst_ref):
    tq = src_ref.shape[0]
    lane = lax.broadcasted_iota(jnp.int32, (tq, PAIR), 1)
    for h in range(N_HEADS):
        pair = src_ref[:, (h // 2) * PAIR:(h // 2 + 1) * PAIR]
        mine = (lane >= HEAD_DIM) if h % 2 else (lane < HEAD_DIM)
        dst_ref[h] = jnp.where(mine, pair, jnp.zeros_like(pair))


def _dsa_kernel(qa_ref, qi_ref, wi_ref, ki_ref, ka_ref, va_ref, o_ref,
                keys_ref, qam_ref, qim_ref, wb_ref, m_ref, l_ref, acc_ref, cut_ref,
                *, tq, tk, n_valid, q_off, topk):
    i = pl.program_id(1)
    q0 = q_off + i * tq
    qpos = q0 + lax.broadcasted_iota(jnp.int32, (tq, 1), 0)
    lim = jnp.minimum(((qpos >> 6) + 1) << 6, n_valid)
    n_adm = jnp.minimum((((q0 + tq - 1) >> 6) + 1) << 6, n_valid)
    nkb = (n_adm + tk - 1) // tk
    reps = tk // LANES
    lane_kpos = lax.broadcasted_iota(jnp.int32, (1, tk), 1)

    _mask_heads(qi_ref, qim_ref)
    _mask_heads(qa_ref, qam_ref)
    for h in range(N_HEADS):
        wb_ref[h] = jnp.broadcast_to(wi_ref[:, IDX_DIM + h:IDX_DIM + h + 1], (tq, LANES))

    def score_block(j, carry):
        ks = pl.multiple_of(j * tk, tk)
        kib = ki_ref[pl.ds(ks, tk), :]
        score = jnp.zeros((tq, tk), F32)
        for h in range(N_HEADS):
            s = lax.dot_general(qim_ref[h], kib, _NT, preferred_element_type=F32)
            wb = wb_ref[h]
            if reps > 1:
                wb = jnp.concatenate([wb] * reps, axis=1)
            score = score + jnp.maximum(s, 0.0) * wb
        kpos = ks + lane_kpos
        keys_ref[:, pl.ds(ks, tk)] = jnp.where(kpos < lim, _sort_key(score), INT_MIN)
        return carry

    lax.fori_loop(0, nkb, score_block, 0)

    def count(pred):
        def body(c, acc):
            cs = pl.multiple_of(c * tk, tk)
            blk = keys_ref[:, pl.ds(cs, tk)]
            for u in range(reps):
                acc = acc + jnp.where(pred(blk[:, u * LANES:(u + 1) * LANES], cs + u * LANES),
                                      1.0, 0.0)
            return acc
        acc = lax.fori_loop(0, nkb, body, jnp.zeros((tq, LANES), F32))
        return jnp.sum(acc, axis=1, keepdims=True)

    kf = float(topk)
    nonneg = count(lambda blk, _: blk >= 0) >= kf
    thr0 = jnp.where(nonneg, jnp.zeros((tq, LANES), jnp.int32),
                     jnp.full((tq, LANES), INT_MIN, jnp.int32))

    def bit_step(t, thr):
        cand = thr | (jnp.int32(1) << (30 - t))
        return jnp.where(count(lambda blk, _: blk >= cand) >= kf, cand, thr)

    thr = lax.fori_loop(0, 31, bit_step, thr0)

    c_gt = count(lambda blk, _: blk > thr)
    c_eq = count(lambda blk, _: blk == thr)
    need = kf - c_gt
    has_thr = thr[:, :1] != INT_MIN
    cut_ref[...] = jnp.where(has_thr, jnp.int32(2 ** 30), jnp.int32(-1)) + jnp.zeros(
        (tq, LANES), jnp.int32)
    split = jnp.where(has_thr & (c_eq > need), 1.0, 0.0)

    @pl.when(jnp.max(split) > 0.0)
    def _():
        lane = lax.broadcasted_iota(jnp.int32, (tq, LANES), 1)

        def idx_step(t, cut):
            cand = cut | (jnp.int32(1) << (13 - t))
            below = count(lambda blk, base: (blk == thr) & (base + lane < cand))
            return jnp.where(below <= need - 1.0, cand, cut)

        cut = lax.fori_loop(0, 14, idx_step, jnp.zeros((tq, LANES), jnp.int32))
        cut_ref[...] = jnp.where(has_thr, cut, -1)

    cut = cut_ref[...]

    m_ref[...] = jnp.full(m_ref.shape, NEG_BIG, F32)
    l_ref[...] = jnp.zeros(l_ref.shape, F32)
    acc_ref[...] = jnp.zeros(acc_ref.shape, F32)

    def attend_block(j, carry):
        ks = pl.multiple_of(j * tk, tk)
        keyb = keys_ref[:, pl.ds(ks, tk)]
        thr_b = jnp.concatenate([thr] * reps, axis=1) if reps > 1 else thr
        cut_b = jnp.concatenate([cut] * reps, axis=1) if reps > 1 else cut
        kpos = ks + lane_kpos
        sel = (keyb > thr_b) | ((keyb == thr_b) & (kpos <= cut_b))
        bias = jnp.where(sel, 0.0, NEG_BIG)
        kab = ka_ref[pl.ds(ks, tk), :]
        vab = va_ref[pl.ds(ks, tk), :]
        for h in range(N_HEADS):
            cols = slice((h // 2) * PAIR, (h // 2 + 1) * PAIR)
            s = lax.dot_general(qam_ref[h], kab[:, cols], _NT, preferred_element_type=F32) + bias
            m_old = m_ref[h]
            m_new = jnp.maximum(m_old, jnp.max(s, axis=1, keepdims=True))
            alpha = jnp.exp(m_old - m_new)
            p = jnp.exp(s - m_new[:, :1])
            l_ref[h] = alpha * l_ref[h] + jnp.sum(p, axis=1, keepdims=True)
            acc_ref[h] = alpha * acc_ref[h] + jnp.dot(p.astype(BF16), vab[:, cols],
                                                      preferred_element_type=F32)
            m_ref[h] = m_new
        return carry

    lax.fori_loop(0, nkb, attend_block, 0)

    lane = lax.broadcasted_iota(jnp.int32, (tq, PAIR), 1)
    for pr in range(N_PAIRS):
        lo = acc_ref[2 * pr] / l_ref[2 * pr]
        hi = acc_ref[2 * pr + 1] / l_ref[2 * pr + 1]
        o_ref[:, pr * PAIR:(pr + 1) * PAIR] = jnp.where(lane < HEAD_DIM, lo, hi).astype(BF16)


def _dsa(qa16, qi16, wi, ki16, ka16, va16, *, tq, tk, n_valid, q_off):
    b, lq, _ = qa16.shape
    lk = ka16.shape[1]
    topk = min(TOPK_MAX, n_valid // 4)
    qblk = lambda bb, i: (bb, i, 0)
    kblk = lambda bb, i: (bb, 0, 0)
    kern = functools.partial(_dsa_kernel, tq=tq, tk=tk, n_valid=n_valid, q_off=q_off, topk=topk)
    return pl.pallas_call(
        kern,
        grid=(b, lq // tq),
        in_specs=[pl.BlockSpec((None, tq, WIDTH), qblk),
                  pl.BlockSpec((None, tq, WIDTH), qblk),
                  pl.BlockSpec((None, tq, LANES), qblk),
                  _resident((None, lk, LANES), kblk),
                  _resident((None, lk, WIDTH), kblk),
                  _resident((None, lk, WIDTH), kblk)],
        out_specs=pl.BlockSpec((None, tq, WIDTH), qblk),
        out_shape=jax.ShapeDtypeStruct((b, lq, WIDTH), BF16),
        scratch_shapes=[pltpu.VMEM((tq, lk), jnp.int32),
                        pltpu.VMEM((N_HEADS, tq, PAIR), BF16),
                        pltpu.VMEM((N_HEADS, tq, PAIR), BF16),
                        pltpu.VMEM((N_HEADS, tq, LANES), F32),
                        pltpu.VMEM((N_HEADS, tq, LANES), F32),
                        pltpu.VMEM((N_HEADS, tq, LANES), F32),
                        pltpu.VMEM((N_HEADS, tq, PAIR), F32),
                        pltpu.VMEM((tq, LANES), jnp.int32)],
        compiler_params=pltpu.CompilerParams(dimension_semantics=("arbitrary", "arbitrary"),
                                             vmem_limit_bytes=VMEM_LIMIT),
        name="dsa",
    )(qa16, qi16, wi, ki16, ka16, va16)


def _sb_kernel(qb_ref, kb_ref, vb_ref, tri_ref, o_ref, qbm_ref, tail_ref, acc_ref,
               *, tq, tk, q_off):
    i = pl.program_id(1)
    q0 = q_off + i * tq
    qpos = q0 + lax.broadcasted_iota(jnp.int32, (tq, 1), 0)
    lane_kpos = lax.broadcasted_iota(jnp.int32, (1, tk), 1)
    tri = tri_ref[...]

    _mask_heads(qb_ref, qbm_ref)
    tail_ref[...] = jnp.zeros(tail_ref.shape, F32)
    acc_ref[...] = jnp.zeros(acc_ref.shape, F32)

    def block(state):
        j, _ = state
        ks = pl.multiple_of(j * tk, tk)
        causal = (ks + lane_kpos) < qpos
        kbb = kb_ref[pl.ds(ks, tk), :]
        vbb = vb_ref[pl.ds(ks, tk), :]
        tail_min = jnp.full((tq, 1), jnp.inf, F32)
        for h in range(N_HEADS):
            cols = slice((h // 2) * PAIR, (h // 2 + 1) * PAIR)
            z = lax.dot_general(qbm_ref[h], kbb[:, cols], _NT, preferred_element_type=F32)
            lp = jnp.log1p(jnp.exp(-jnp.abs(z)))
            sp = jnp.where(causal, jnp.maximum(z, 0.0) + lp, 0.0)
            hi = sp.astype(BF16)
            lo = (sp - hi.astype(F32)).astype(BF16)
            inner = (jnp.dot(hi, tri, preferred_element_type=F32)
                     + jnp.dot(lo, tri, preferred_element_type=F32))
            tail_old = tail_ref[h]
            log_a = (jnp.minimum(z, 0.0) - lp) - inner - tail_old[:, :1]
            a = jnp.where(causal, jnp.exp(log_a), 0.0)
            acc_ref[h] = acc_ref[h] + jnp.dot(a.astype(BF16), vbb[:, cols],
                                              preferred_element_type=F32)
            tail_new = tail_old + jnp.sum(sp, axis=1, keepdims=True)
            tail_ref[h] = tail_new
            tail_min = jnp.minimum(tail_min, tail_new[:, :1])
        return j - 1, jnp.min(tail_min)

    def more(state):
        j, tail_min = state
        return (j >= 0) & (tail_min <= SB_EXIT_TAIL)

    lax.while_loop(more, block, ((q0 + tq - 1) // tk, jnp.float32(0.0)))

    lane = lax.broadcasted_iota(jnp.int32, (tq, PAIR), 1)
    for pr in range(N_PAIRS):
        o_ref[:, pr * PAIR:(pr + 1) * PAIR] = jnp.where(
            lane < HEAD_DIM, acc_ref[2 * pr], acc_ref[2 * pr + 1]).astype(BF16)


def _sb(qb16, kb16, vb16, tri, *, tq, tk, q_off):
    b, lq, _ = qb16.shape
    lk = kb16.shape[1]
    qblk = lambda bb, i: (bb, i, 0)
    kblk = lambda bb, i: (bb, 0, 0)
    kern = functools.partial(_sb_kernel, tq=tq, tk=tk, q_off=q_off)
    return pl.pallas_call(
        kern,
        grid=(b, lq // tq),
        in_specs=[pl.BlockSpec((None, tq, WIDTH), qblk),
                  _resident((None, lk, WIDTH), kblk),
                  _resident((None, lk, WIDTH), kblk),
                  _resident((tk, tk), lambda bb, i: (0, 0))],
        out_specs=pl.BlockSpec((None, tq, WIDTH), qblk),
        out_shape=jax.ShapeDtypeStruct((b, lq, WIDTH), BF16),
        scratch_shapes=[pltpu.VMEM((N_HEADS, tq, PAIR), BF16),
                        pltpu.VMEM((N_HEADS, tq, LANES), F32),
                        pltpu.VMEM((N_HEADS, tq, PAIR), F32)],
        compiler_params=pltpu.CompilerParams(dimension_semantics=("arbitrary", "arbitrary"),
                                             vmem_limit_bytes=VMEM_LIMIT),
        name="sb",
    )(qb16, kb16, vb16, tri)


FF_CHUNK = 1024


def _rms_rows(x, g):
    return x * lax.rsqrt(jnp.mean(x * x, axis=-1, keepdims=True) + EPS) * g


def _merge_kernel(x_ref, oa_ref, ob_ref, gmix_ref, gffn_ref, wg_ref, wba_ref, wbb_ref,
                  wout_ref, wup_ref, wdown_ref, y_ref):
    x = x_ref[...]
    hx = _rms_rows(x, gmix_ref[...]).astype(BF16)
    gate_a = jnp.dot(hx, wg_ref[:, :D_MODEL], preferred_element_type=F32)
    gate_b = jnp.dot(hx, wg_ref[:, D_MODEL:], preferred_element_type=F32)
    pa = jnp.dot(oa_ref[...], wba_ref[...], preferred_element_type=F32)
    pb = jnp.dot(ob_ref[...], wbb_ref[...], preferred_element_type=F32)
    m = jax.nn.sigmoid(gate_a) * pa + jax.nn.sigmoid(gate_b) * pb
    h = x + jnp.dot(m.astype(BF16), wout_ref[...], preferred_element_type=F32)
    hn = _rms_rows(h, gffn_ref[...]).astype(BF16)
    y = h
    for c in range(D_FF // FF_CHUNK):
        u = jnp.dot(hn, wup_ref[:, c * FF_CHUNK:(c + 1) * FF_CHUNK], preferred_element_type=F32)
        r = jnp.square(jnp.maximum(u, 0.0)).astype(BF16)
        y = y + jnp.dot(r, wdown_ref[c * FF_CHUNK:(c + 1) * FF_CHUNK, :],
                        preferred_element_type=F32)
    y_ref[...] = y


def _merge_ffn(x2d, oa16, ob16, wts, tm):
    n = x2d.shape[0]
    row = lambda i: (i, 0)
    const = lambda i: (0, 0)
    weights = [wts["w_gate"], wts["w_branch_a"], wts["w_branch_b"], wts["w_out"],
               wts["w_up"], wts["w_down"]]
    return pl.pallas_call(
        _merge_kernel,
        grid=(n // tm,),
        in_specs=[pl.BlockSpec((tm, D_MODEL), row),
                  pl.BlockSpec((tm, WIDTH), row),
                  pl.BlockSpec((tm, WIDTH), row),
                  pl.BlockSpec((1, D_MODEL), const),
                  pl.BlockSpec((1, D_MODEL), const)]
                 + [_resident(w.shape, const) for w in weights],
        out_specs=pl.BlockSpec((tm, D_MODEL), row),
        out_shape=jax.ShapeDtypeStruct((n, D_MODEL), F32),
        compiler_params=pltpu.CompilerParams(dimension_semantics=("arbitrary",),
                                             vmem_limit_bytes=VMEM_LIMIT),
        name="merge_ffn",
    )(x2d, oa16, ob16, wts["g_mix"], wts["g_ffn"], *weights)


def _rope_tables(pos, rows):
    half = HEAD_DIM // 2
    inv_freq = jnp.power(ROPE_THETA, -jnp.arange(half, dtype=F32) / half)
    ang = pos.astype(F32)[:, None] * inv_freq[None, :]
    cos, sin = jnp.cos(ang), jnp.sin(ang)
    cs = jnp.concatenate([cos, cos, cos, cos], axis=1)
    sn = jnp.concatenate([-sin, sin, -sin, sin], axis=1)
    reps = rows // pos.shape[0]
    if reps > 1:
        cs, sn = jnp.tile(cs, (reps, 1)), jnp.tile(sn, (reps, 1))
    return cs, sn


def _prep_weights(g_mix, w_in, g_qn, g_kn, w_branch_a, w_branch_b, w_out, g_ffn, w_up, w_down):
    offs = [int(v) for v in np.cumsum(SPLIT_SIZES)[:-1]]
    w_qa, w_ka, w_va, w_qi, w_ki, w_wi, w_qb, w_kb, w_vb, w_ga, w_gb = jnp.split(w_in, offs, axis=1)
    pad = jnp.zeros((D_MODEL, LANES - IDX_DIM - N_HEADS), w_in.dtype)
    w_proj = jnp.concatenate([w_qa, w_ka, w_va, w_qi, w_qb, w_kb, w_vb, w_ki, w_wi, pad], axis=1)
    head = np.arange(WIDTH) // HEAD_DIM
    return {
        "w_proj": w_proj.astype(BF16),
        "w_gate": jnp.concatenate([w_ga, w_gb], axis=1).astype(BF16),
        "grp": jnp.asarray(head[:, None] == head[None, :], BF16),
        "g_mix": g_mix.reshape(1, D_MODEL),
        "g_ffn": g_ffn.reshape(1, D_MODEL),
        "g_qn": jnp.tile(g_qn, N_HEADS).reshape(1, WIDTH),
        "g_kn": jnp.tile(g_kn, N_HEADS).reshape(1, WIDTH),
        "w_branch_a": w_branch_a.astype(BF16),
        "w_branch_b": w_branch_b.astype(BF16),
        "w_out": w_out.astype(BF16),
        "w_up": w_up.astype(BF16),
        "w_down": w_down.astype(BF16),
    }


def _tri(tk):
    r = np.arange(tk)
    return jnp.asarray(r[:, None] > r[None, :], BF16)


def _pad_keys(a, lk):
    return jnp.pad(a, ((0, 0), (0, lk - a.shape[1]), (0, 0)))


def _layer(x, pos, wts, caches, *, tq_dsa, tk_dsa, tq_sb, tk_sb):
    b, l, _ = x.shape
    n = b * l
    tm = min(512, n)
    x2d = x.reshape(n, D_MODEL)
    rows = l if l >= tm else tm
    (ka, va, ki, kb, vb, qa16, ka16, va16, qi16, ki16, qb16, kb16, vb16, wi) = _project(
        x2d, _rope_tables(pos, rows), wts, tm)
    per_seq = lambda a: a.reshape(b, l, a.shape[-1])
    qa16, qi16, qb16, wi = per_seq(qa16), per_seq(qi16), per_seq(qb16), per_seq(wi)
    ka16, va16, ki16, kb16, vb16 = (per_seq(a) for a in (ka16, va16, ki16, kb16, vb16))
    past = 0
    if caches is not None:
        c_ka, c_va, c_ki, c_kb, c_vb = caches
        past = c_ka.shape[1]
        flat = lambda c: c.reshape(b, past, -1).astype(BF16)
        c_ki16 = c_ki.astype(BF16)
        ka16 = jnp.concatenate([flat(c_ka), ka16], axis=1)
        va16 = jnp.concatenate([flat(c_va), va16], axis=1)
        ki16 = jnp.concatenate([jnp.concatenate([c_ki16, c_ki16], axis=2), ki16], axis=1)
        kb16 = jnp.concatenate([flat(c_kb), kb16], axis=1)
        vb16 = jnp.concatenate([flat(c_vb), vb16], axis=1)
    n_valid = past + l
    lk_dsa = pl.cdiv(n_valid, tk_dsa) * tk_dsa
    lk_sb = pl.cdiv(n_valid, tk_sb) * tk_sb
    oa16 = _dsa(qa16, qi16, wi, _pad_keys(ki16, lk_dsa), _pad_keys(ka16, lk_dsa),
                _pad_keys(va16, lk_dsa), tq=tq_dsa, tk=tk_dsa, n_valid=n_valid, q_off=past)
    ob16 = _sb(qb16, _pad_keys(kb16, lk_sb), _pad_keys(vb16, lk_sb), _tri(tk_sb),
               tq=tq_sb, tk=tk_sb, q_off=past)
    y = _merge_ffn(x2d, oa16.reshape(n, WIDTH), ob16.reshape(n, WIDTH), wts, tm)
    heads = lambda a: a.reshape(b, l, N_HEADS, HEAD_DIM)
    return (y.reshape(b, l, D_MODEL), heads(ka), heads(va), ki.reshape(b, l, IDX_DIM),
            heads(kb), heads(vb))


def kernel(x_prompt, x_sample, cache_k_a, cache_v_a, cache_k_idx, cache_k_sb, cache_v_sb,
           g_mix, w_in, g_qn, g_kn, w_branch_a, w_branch_b, w_out, g_ffn, w_up, w_down):
    wts = _prep_weights(g_mix, w_in, g_qn, g_kn, w_branch_a, w_branch_b, w_out, g_ffn, w_up, w_down)
    s = x_prompt.shape[1]
    t = x_sample.shape[1]
    past = cache_k_a.shape[1]
    prompt = _layer(x_prompt, jnp.arange(s, dtype=jnp.int32), wts, None,
                    tq_dsa=min(128, s), tk_dsa=min(512, s), tq_sb=min(256, s), tk_sb=min(256, s))
    sample = _layer(x_sample, past + jnp.arange(t, dtype=jnp.int32), wts,
                    (cache_k_a, cache_v_a, cache_k_idx, cache_k_sb, cache_v_sb),
                    tq_dsa=t, tk_dsa=128, tq_sb=t, tk_sb=128)
    return (prompt[0], sample[0]) + prompt[1:] + sample[1:]
```

```python
import functools

import numpy as np
import jax
import jax.numpy as jnp
from jax import lax
from jax.experimental import pallas as pl
from jax.experimental.pallas import tpu as pltpu

D_MODEL = 1024
HEAD_DIM = 64
N_HEADS = 8
WIDTH = N_HEADS * HEAD_DIM
IDX_DIM = 64
CHUNK = 64
CHUNK_SHIFT = CHUNK.bit_length() - 1
assert 1 << CHUNK_SHIFT == CHUNK
TOPK_MAX = 256
D_FF = 4 * D_MODEL
ROPE_THETA = 10000.0
EPS = 1e-6
SPLIT_SIZES = (WIDTH, WIDTH, WIDTH, N_HEADS * IDX_DIM, IDX_DIM, N_HEADS,
               WIDTH, WIDTH, WIDTH, D_MODEL, D_MODEL)

LANES = 128
PAIR = 2 * HEAD_DIM
N_PAIRS = N_HEADS // 2
QK_SCALE = HEAD_DIM ** -0.5
W_IDX_SCALE = (N_HEADS ** -0.5) * (IDX_DIM ** -0.5)
INT_MIN = -2 ** 31
NEG_BIG = -1e30
SOFTMAX_SAFE_BOUND = 40.0
SB_EXIT_TAIL = 105.0
VMEM_LIMIT = 56 * 1024 * 1024

F32 = jnp.float32
BF16 = jnp.bfloat16
_NT = (((1,), (1,)), ((), ()))


def _resident(shape, index_map):
    return pl.BlockSpec(shape, index_map, pipeline_mode=pl.Buffered(1))


def _rope(x, cs, sn):
    w = x.shape[1]
    lane = lax.broadcasted_iota(jnp.int32, x.shape, 1)
    first_half = (lane & (HEAD_DIM // 2)) == 0
    partner = jnp.where(first_half, pltpu.roll(x, w - HEAD_DIM // 2, 1),
                        pltpu.roll(x, HEAD_DIM // 2, 1))
    reps = w // LANES
    if reps > 1:
        cs = jnp.concatenate([cs] * reps, axis=1)
        sn = jnp.concatenate([sn] * reps, axis=1)
    return x * cs + partner * sn


def _head_rmsnorm(z, grp, g):
    zz = z * z
    hi = zz.astype(BF16)
    lo = (zz - hi.astype(F32)).astype(BF16)
    ss = (jnp.dot(hi, grp, preferred_element_type=F32)
          + jnp.dot(lo, grp, preferred_element_type=F32))
    return z * lax.rsqrt(ss * (1.0 / HEAD_DIM) + EPS) * g


def _proj_kernel(x_ref, gmix_ref, w_ref, gqn_ref, gkn_ref, cs_ref, sn_ref, grp_ref,
                 ka_ref, va_ref, ki_ref, kb_ref, vb_ref,
                 qa16_ref, ka16_ref, va16_ref, qi16_ref, ki16_ref,
                 qb16_ref, kb16_ref, vb16_ref, wi_ref):
    x = x_ref[...]
    h = (x * lax.rsqrt(jnp.mean(x * x, axis=-1, keepdims=True) + EPS)
         * gmix_ref[...]).astype(BF16)
    cs = cs_ref[...]
    sn = sn_ref[...]
    grp = grp_ref[...]

    def seg(k, width=WIDTH):
        return jnp.dot(h, w_ref[:, k * WIDTH:k * WIDTH + width],
                       preferred_element_type=F32)

    qa = _rope(_head_rmsnorm(seg(0), grp, gqn_ref[...]), cs, sn)
    qa16_ref[...] = (qa * QK_SCALE).astype(BF16)
    ka = _rope(_head_rmsnorm(seg(1), grp, gkn_ref[...]), cs, sn)
    ka_ref[...] = ka
    ka16_ref[...] = ka.astype(BF16)
    va = seg(2)
    va_ref[...] = va
    va16_ref[...] = va.astype(BF16)
    qi16_ref[...] = _rope(seg(3), cs, sn).astype(BF16)
    qb16_ref[...] = (seg(4) * QK_SCALE).astype(BF16)
    kb = seg(5)
    kb_ref[...] = kb
    kb16_ref[...] = kb.astype(BF16)
    vb = seg(6)
    vb_ref[...] = vb
    vb16_ref[...] = vb.astype(BF16)
    misc = seg(7, LANES)
    ki = _rope(misc, cs, sn)[:, :IDX_DIM]
    ki_ref[...] = ki
    ki16 = ki.astype(BF16)
    ki16_ref[...] = jnp.concatenate([ki16, ki16], axis=1)
    wi_ref[...] = misc * W_IDX_SCALE


def _project(x2d, tabs, wts, tm):
    n = x2d.shape[0]
    cs, sn = tabs
    nt = cs.shape[0] // tm
    row = lambda i: (i, 0)
    const = lambda i: (0, 0)
    tab = lambda i: (i % nt, 0)
    f32_leaf = lambda w: jax.ShapeDtypeStruct((n, w), F32)
    b16 = lambda w: jax.ShapeDtypeStruct((n, w), BF16)
    out_shape = [f32_leaf(WIDTH), f32_leaf(WIDTH), f32_leaf(IDX_DIM), f32_leaf(WIDTH), f32_leaf(WIDTH),
                 b16(WIDTH), b16(WIDTH), b16(WIDTH), b16(WIDTH), b16(LANES),
                 b16(WIDTH), b16(WIDTH), b16(WIDTH), f32_leaf(LANES)]
    out_specs = [pl.BlockSpec((tm, s.shape[1]), row) for s in out_shape]
    w_proj = wts["w_proj"]
    return pl.pallas_call(
        _proj_kernel,
        grid=(n // tm,),
        in_specs=[pl.BlockSpec((tm, D_MODEL), row),
                  pl.BlockSpec((1, D_MODEL), const),
                  _resident(w_proj.shape, const),
                  pl.BlockSpec((1, WIDTH), const),
                  pl.BlockSpec((1, WIDTH), const),
                  pl.BlockSpec((tm, LANES), tab),
                  pl.BlockSpec((tm, LANES), tab),
                  _resident((WIDTH, WIDTH), const)],
        out_specs=out_specs,
        out_shape=out_shape,
        compiler_params=pltpu.CompilerParams(dimension_semantics=("arbitrary",),
                                             vmem_limit_bytes=VMEM_LIMIT),
        name="proj",
    )(x2d, wts["g_mix"], w_proj, wts["g_qn"], wts["g_kn"], cs, sn, wts["grp"])


ROW_GROUP = 128


def _key_to_float(k):
    return lax.bitcast_convert_type(k ^ ((k >> 31) & 0x7FFFFFFF), F32)


def _mask_heads(src_ref, dst_ref):
    tq = src_ref.shape[0]
    lane = lax.broadcasted_iota(jnp.int32, (tq, PAIR), 1)
    for p in range(N_PAIRS):
        pair = src_ref[:, p * PAIR:(p + 1) * PAIR]
        zero = jnp.zeros_like(pair)
        dst_ref[p, :tq, :] = jnp.where(lane < HEAD_DIM, pair, zero)
        dst_ref[p, tq:, :] = jnp.where(lane >= HEAD_DIM, pair, zero)


def _lanes(x, reps):
    return jnp.concatenate([x] * reps, axis=1) if reps > 1 else x


def _dsa_kernel(bound_ref, qa_ref, qi_ref, wi_ref, ki_ref, ka_ref, va_ref, o_ref,
                sc_ref, qam_ref, qim_ref, wb_ref, shift_ref, l_ref, acc_ref,
                key_ref, cand_ref, thr_ref, cut_ref, *, tq, tk, n_valid, q_off, topk):
    i = pl.program_id(1)
    q0 = q_off + i * tq
    qpos = q0 + lax.broadcasted_iota(jnp.int32, (tq, 1), 0)
    chunk_end = lambda pos: ((pos >> CHUNK_SHIFT) + 1) << CHUNK_SHIFT
    lim = jnp.minimum(chunk_end(qpos), n_valid)
    n_adm = jnp.minimum(chunk_end(q0 + tq - 1), n_valid)
    nkb = (n_adm + tk - 1) // tk
    reps = tk // LANES
    rg = min(ROW_GROUP, tq)
    lane_kpos = lax.broadcasted_iota(jnp.int32, (1, tk), 1)

    _mask_heads(qi_ref, qim_ref)
    _mask_heads(qa_ref, qam_ref)
    for h in range(N_HEADS):
        wb_ref[h // 2, (h % 2) * tq:(h % 2 + 1) * tq, :] = jnp.broadcast_to(
            wi_ref[:, IDX_DIM + h:IDX_DIM + h + 1], (tq, LANES))

    def score_block(j, carry):
        ks = pl.multiple_of(j * tk, tk)
        kib = ki_ref[pl.ds(ks, tk), :]
        score = jnp.zeros((tq, tk), F32)
        for p in range(N_PAIRS):
            s = lax.dot_general(qim_ref[p], kib, _NT, preferred_element_type=F32)
            r = jnp.maximum(s, 0.0) * _lanes(wb_ref[p], reps)
            score = score + r[:tq] + r[tq:]
        sc_ref[:, pl.ds(ks, tk)] = jnp.where(ks + lane_kpos < lim, score, -jnp.inf)
        return carry

    lax.fori_loop(0, nkb, score_block, 0)

    def count(pred):
        def body(c, accs):
            cs = pl.multiple_of(c * tk, tk)
            out = []
            for g, acc in enumerate(accs):
                blk = sc_ref[g * rg:(g + 1) * rg, pl.ds(cs, tk)]
                for u in range(reps):
                    hit = pred(blk[:, u * LANES:(u + 1) * LANES], g * rg, cs + u * LANES)
                    acc = acc + jnp.where(hit, 1.0, 0.0)
                out.append(acc)
            return tuple(out)
        accs = lax.fori_loop(0, nkb, body,
                             tuple(jnp.zeros((rg, LANES), F32) for _ in range(tq // rg)))
        return jnp.concatenate([jnp.sum(a, axis=1, keepdims=True) for a in accs], axis=0)

    rows = lambda ref, r0: ref[r0:r0 + rg, :]
    kf = float(topk)
    nonneg = count(lambda blk, r0, c0: blk >= 0.0) >= kf
    key_ref[...] = jnp.where(nonneg, jnp.zeros((tq, LANES), jnp.int32),
                             jnp.full((tq, LANES), INT_MIN, jnp.int32))

    def bit_step(t, carry):
        cand = key_ref[...] | (jnp.int32(1) << (30 - t))
        cand_ref[...] = _key_to_float(cand)
        take = count(lambda blk, r0, c0: blk >= rows(cand_ref, r0)) >= kf
        key_ref[...] = jnp.where(take, cand, key_ref[...])
        return carry

    lax.fori_loop(0, 31, bit_step, 0)
    key = key_ref[...]
    has_thr = key != INT_MIN
    thr_ref[...] = jnp.where(has_thr, _key_to_float(key), -jnp.inf)
    c_gt = count(lambda blk, r0, c0: blk > rows(thr_ref, r0))
    c_eq = count(lambda blk, r0, c0: blk == rows(thr_ref, r0))
    need = kf - c_gt
    cut_ref[...] = jnp.where(has_thr, jnp.int32(2 ** 30), jnp.int32(-1))
    split = has_thr[:, :1] & (c_eq > need)

    @pl.when(jnp.max(jnp.where(split, 1.0, 0.0)) > 0.0)
    def _():
        lane = lax.broadcasted_iota(jnp.int32, (rg, LANES), 1)
        key_ref[...] = jnp.zeros((tq, LANES), jnp.int32)

        def idx_step(t, carry):
            cand = key_ref[...] | (jnp.int32(1) << (13 - t))
            cut_ref[...] = cand
            below = count(lambda blk, r0, c0: (blk == rows(thr_ref, r0))
                          & (c0 + lane < rows(cut_ref, r0)))
            key_ref[...] = jnp.where(below <= need - 1.0, cand, key_ref[...])
            return carry

        lax.fori_loop(0, 14, idx_step, 0)
        cut_ref[...] = jnp.where(thr_ref[...] > -jnp.inf, key_ref[...], -1)

    def masked_bias(ks):
        scb = sc_ref[:, pl.ds(ks, tk)]
        thr_b = _lanes(thr_ref[...], reps)
        cut_b = _lanes(cut_ref[...], reps)
        sel = (scb > thr_b) | ((scb == thr_b) & (ks + lane_kpos <= cut_b))
        bias = jnp.where(sel, 0.0, NEG_BIG)
        return jnp.concatenate([bias, bias], axis=0)

    def logits(p, kab):
        return lax.dot_general(qam_ref[p], kab[:, p * PAIR:(p + 1) * PAIR], _NT,
                               preferred_element_type=F32)

    bound = bound_ref[0]
    shift_ref[...] = jnp.full(shift_ref.shape, bound, F32)

    @pl.when(bound > SOFTMAX_SAFE_BOUND)
    def _():
        l_ref[...] = jnp.full(l_ref.shape, NEG_BIG, F32)

        def max_block(j, carry):
            ks = pl.multiple_of(j * tk, tk)
            bias = masked_bias(ks)
            kab = ka_ref[pl.ds(ks, tk), :]
            for p in range(N_PAIRS):
                s = logits(p, kab) + bias
                mx = l_ref[p]
                for u in range(reps):
                    mx = jnp.maximum(mx, s[:, u * LANES:(u + 1) * LANES])
                l_ref[p] = mx
            return carry

        lax.fori_loop(0, nkb, max_block, 0)
        for p in range(N_PAIRS):
            shift_ref[p] = jnp.broadcast_to(jnp.max(l_ref[p], axis=1, keepdims=True),
                                            (2 * tq, LANES))

    l_ref[...] = jnp.zeros(l_ref.shape, F32)
    acc_ref[...] = jnp.zeros(acc_ref.shape, F32)

    def attend_block(j, carry):
        ks = pl.multiple_of(j * tk, tk)
        bias = masked_bias(ks)
        kab = ka_ref[pl.ds(ks, tk), :]
        vab = va_ref[pl.ds(ks, tk), :]
        for p in range(N_PAIRS):
            e = jnp.exp((logits(p, kab) - _lanes(shift_ref[p], reps)) + bias)
            part = l_ref[p]
            for u in range(reps):
                part = part + e[:, u * LANES:(u + 1) * LANES]
            l_ref[p] = part
            acc_ref[p] = acc_ref[p] + jnp.dot(
                e.astype(BF16), vab[:, p * PAIR:(p + 1) * PAIR], preferred_element_type=F32)
        return carry

    lax.fori_loop(0, nkb, attend_block, 0)

    lane_o = lax.broadcasted_iota(jnp.int32, (tq, PAIR), 1)
    for p in range(N_PAIRS):
        out = acc_ref[p] / jnp.sum(l_ref[p], axis=1, keepdims=True)
        o_ref[:, p * PAIR:(p + 1) * PAIR] = jnp.where(
            lane_o < HEAD_DIM, out[:tq], out[tq:]).astype(BF16)


def _dsa(bound, qa16, qi16, wi, ki16, ka16, va16, *, tq, tk, n_valid, q_off):
    b, lq, _ = qa16.shape
    lk = ka16.shape[1]
    topk = min(TOPK_MAX, n_valid // 4)
    qblk = lambda bb, i: (bb, i, 0)
    kblk = lambda bb, i: (bb, 0, 0)
    kern = functools.partial(_dsa_kernel, tq=tq, tk=tk, n_valid=n_valid, q_off=q_off, topk=topk)
    stacked = lambda w, dt: pltpu.VMEM((N_PAIRS, 2 * tq, w), dt)
    return pl.pallas_call(
        kern,
        grid=(b, lq // tq),
        in_specs=[pl.BlockSpec(memory_space=pltpu.SMEM),
                  pl.BlockSpec((None, tq, WIDTH), qblk),
                  pl.BlockSpec((None, tq, WIDTH), qblk),
                  pl.BlockSpec((None, tq, LANES), qblk),
                  _resident((None, lk, LANES), kblk),
                  _resident((None, lk, WIDTH), kblk),
                  _resident((None, lk, WIDTH), kblk)],
        out_specs=pl.BlockSpec((None, tq, WIDTH), qblk),
        out_shape=jax.ShapeDtypeStruct((b, lq, WIDTH), BF16),
        scratch_shapes=[pltpu.VMEM((tq, lk), F32),
                        stacked(PAIR, BF16),
                        stacked(PAIR, BF16),
                        stacked(LANES, F32),
                        stacked(LANES, F32),
                        stacked(LANES, F32),
                        stacked(PAIR, F32),
                        pltpu.VMEM((tq, LANES), jnp.int32),
                        pltpu.VMEM((tq, LANES), F32),
                        pltpu.VMEM((tq, LANES), F32),
                        pltpu.VMEM((tq, LANES), jnp.int32)],
        compiler_params=pltpu.CompilerParams(dimension_semantics=("arbitrary", "arbitrary"),
                                             vmem_limit_bytes=VMEM_LIMIT),
        name="dsa",
    )(bound, qa16, qi16, wi, ki16, ka16, va16)


def _sb_kernel(qb_ref, kb_ref, vb_ref, tri_ref, o_ref, qbm_ref, tail_ref, acc_ref,
               *, tq, tk, q_off):
    i = pl.program_id(1)
    q0 = q_off + i * tq
    qpos = q0 + lax.broadcasted_iota(jnp.int32, (tq, 1), 0)
    qpos = jnp.concatenate([qpos, qpos], axis=0)
    lane_kpos = lax.broadcasted_iota(jnp.int32, (1, tk), 1)
    tri = tri_ref[...]

    _mask_heads(qb_ref, qbm_ref)
    tail_ref[...] = jnp.zeros(tail_ref.shape, F32)
    acc_ref[...] = jnp.zeros(acc_ref.shape, F32)

    def block(state):
        j, _ = state
        ks = pl.multiple_of(j * tk, tk)
        causal = (ks + lane_kpos) < qpos
        kbb = kb_ref[pl.ds(ks, tk), :]
        vbb = vb_ref[pl.ds(ks, tk), :]
        tail_min = jnp.full((2 * tq, 1), jnp.inf, F32)
        for p in range(N_PAIRS):
            cols = slice(p * PAIR, (p + 1) * PAIR)
            z = lax.dot_general(qbm_ref[p], kbb[:, cols], _NT, preferred_element_type=F32)
            lp = jnp.log1p(jnp.exp(-jnp.abs(z)))
            sp = jnp.where(causal, jnp.maximum(z, 0.0) + lp, 0.0)
            hi = sp.astype(BF16)
            lo = (sp - hi.astype(F32)).astype(BF16)
            inner = (jnp.dot(hi, tri, preferred_element_type=F32)
                     + jnp.dot(lo, tri, preferred_element_type=F32))
            tail_old = tail_ref[p]
            log_a = (jnp.minimum(z, 0.0) - lp) - inner - tail_old[:, :1]
            a = jnp.where(causal, jnp.exp(log_a), 0.0)
            acc_ref[p] = acc_ref[p] + jnp.dot(a.astype(BF16), vbb[:, cols],
                                              preferred_element_type=F32)
            tail_new = tail_old + jnp.sum(sp, axis=1, keepdims=True)
            tail_ref[p] = tail_new
            tail_min = jnp.minimum(tail_min, tail_new[:, :1])
        return j - 1, jnp.min(tail_min)

    def more(state):
        j, tail_min = state
        return (j >= 0) & (tail_min <= SB_EXIT_TAIL)

    lax.while_loop(more, block, ((q0 + tq - 1) // tk, jnp.float32(0.0)))

    lane = lax.broadcasted_iota(jnp.int32, (tq, PAIR), 1)
    for p in range(N_PAIRS):
        acc = acc_ref[p]
        o_ref[:, p * PAIR:(p + 1) * PAIR] = jnp.where(
            lane < HEAD_DIM, acc[:tq], acc[tq:]).astype(BF16)


def _sb(qb16, kb16, vb16, tri, *, tq, tk, q_off):
    b, lq, _ = qb16.shape
    lk = kb16.shape[1]
    qblk = lambda bb, i: (bb, i, 0)
    kblk = lambda bb, i: (bb, 0, 0)
    kern = functools.partial(_sb_kernel, tq=tq, tk=tk, q_off=q_off)
    return pl.pallas_call(
        kern,
        grid=(b, lq // tq),
        in_specs=[pl.BlockSpec((None, tq, WIDTH), qblk),
                  _resident((None, lk, WIDTH), kblk),
                  _resident((None, lk, WIDTH), kblk),
                  _resident((tk, tk), lambda bb, i: (0, 0))],
        out_specs=pl.BlockSpec((None, tq, WIDTH), qblk),
        out_shape=jax.ShapeDtypeStruct((b, lq, WIDTH), BF16),
        scratch_shapes=[pltpu.VMEM((N_PAIRS, 2 * tq, PAIR), BF16),
                        pltpu.VMEM((N_PAIRS, 2 * tq, LANES), F32),
                        pltpu.VMEM((N_PAIRS, 2 * tq, PAIR), F32)],
        compiler_params=pltpu.CompilerParams(dimension_semantics=("arbitrary", "arbitrary"),
                                             vmem_limit_bytes=VMEM_LIMIT),
        name="sb",
    )(qb16, kb16, vb16, tri)


FF_CHUNK = 1024


def _rms_rows(x, g):
    return x * lax.rsqrt(jnp.mean(x * x, axis=-1, keepdims=True) + EPS) * g


def _merge_kernel(x_ref, oa_ref, ob_ref, gmix_ref, gffn_ref, wg_ref, wba_ref, wbb_ref,
                  wout_ref, wup_ref, wdown_ref, y_ref):
    x = x_ref[...]
    hx = _rms_rows(x, gmix_ref[...]).astype(BF16)
    gate_a = jnp.dot(hx, wg_ref[:, :D_MODEL], preferred_element_type=F32)
    gate_b = jnp.dot(hx, wg_ref[:, D_MODEL:], preferred_element_type=F32)
    pa = jnp.dot(oa_ref[...], wba_ref[...], preferred_element_type=F32)
    pb = jnp.dot(ob_ref[...], wbb_ref[...], preferred_element_type=F32)
    m = jax.nn.sigmoid(gate_a) * pa + jax.nn.sigmoid(gate_b) * pb
    h = x + jnp.dot(m.astype(BF16), wout_ref[...], preferred_element_type=F32)
    hn = _rms_rows(h, gffn_ref[...]).astype(BF16)
    y = h
    for c in range(D_FF // FF_CHUNK):
        u = jnp.dot(hn, wup_ref[:, c * FF_CHUNK:(c + 1) * FF_CHUNK], preferred_element_type=F32)
        r = jnp.square(jnp.maximum(u, 0.0)).astype(BF16)
        y = y + jnp.dot(r, wdown_ref[c * FF_CHUNK:(c + 1) * FF_CHUNK, :],
                        preferred_element_type=F32)
    y_ref[...] = y


def _merge_ffn(x2d, oa16, ob16, wts, tm):
    n = x2d.shape[0]
    row = lambda i: (i, 0)
    const = lambda i: (0, 0)
    weights = [wts["w_gate"], wts["w_branch_a"], wts["w_branch_b"], wts["w_out"],
               wts["w_up"], wts["w_down"]]
    return pl.pallas_call(
        _merge_kernel,
        grid=(n // tm,),
        in_specs=[pl.BlockSpec((tm, D_MODEL), row),
                  pl.BlockSpec((tm, WIDTH), row),
                  pl.BlockSpec((tm, WIDTH), row),
                  pl.BlockSpec((1, D_MODEL), const),
                  pl.BlockSpec((1, D_MODEL), const)]
                 + [_resident(w.shape, const) for w in weights],
        out_specs=pl.BlockSpec((tm, D_MODEL), row),
        out_shape=jax.ShapeDtypeStruct((n, D_MODEL), F32),
        compiler_params=pltpu.CompilerParams(dimension_semantics=("arbitrary",),
                                             vmem_limit_bytes=VMEM_LIMIT),
        name="merge_ffn",
    )(x2d, oa16, ob16, wts["g_mix"], wts["g_ffn"], *weights)


def _rope_tables(pos, rows):
    half = HEAD_DIM // 2
    inv_freq = jnp.power(ROPE_THETA, -jnp.arange(half, dtype=F32) / half)
    ang = pos.astype(F32)[:, None] * inv_freq[None, :]
    cos, sin = jnp.cos(ang), jnp.sin(ang)
    cs = jnp.concatenate([cos, cos, cos, cos], axis=1)
    sn = jnp.concatenate([-sin, sin, -sin, sin], axis=1)
    reps = rows // pos.shape[0]
    if reps > 1:
        cs, sn = jnp.tile(cs, (reps, 1)), jnp.tile(sn, (reps, 1))
    return cs, sn


def _prep_weights(g_mix, w_in, g_qn, g_kn, w_branch_a, w_branch_b, w_out, g_ffn, w_up, w_down):
    offs = [int(v) for v in np.cumsum(SPLIT_SIZES)[:-1]]
    w_qa, w_ka, w_va, w_qi, w_ki, w_wi, w_qb, w_kb, w_vb, w_ga, w_gb = jnp.split(w_in, offs, axis=1)
    pad = jnp.zeros((D_MODEL, LANES - IDX_DIM - N_HEADS), w_in.dtype)
    w_proj = jnp.concatenate([w_qa, w_ka, w_va, w_qi, w_qb, w_kb, w_vb, w_ki, w_wi, pad], axis=1)
    head = np.arange(WIDTH) // HEAD_DIM
    return {
        "w_proj": w_proj.astype(BF16),
        "w_gate": jnp.concatenate([w_ga, w_gb], axis=1).astype(BF16),
        "grp": jnp.asarray(head[:, None] == head[None, :], BF16),
        "g_mix": g_mix.reshape(1, D_MODEL),
        "g_ffn": g_ffn.reshape(1, D_MODEL),
        "g_qn": jnp.tile(g_qn, N_HEADS).reshape(1, WIDTH),
        "g_kn": jnp.tile(g_kn, N_HEADS).reshape(1, WIDTH),
        "logit_bound": (HEAD_DIM * QK_SCALE * jnp.max(jnp.abs(g_qn))
                        * jnp.max(jnp.abs(g_kn))).reshape(1).astype(F32),
        "w_branch_a": w_branch_a.astype(BF16),
        "w_branch_b": w_branch_b.astype(BF16),
        "w_out": w_out.astype(BF16),
        "w_up": w_up.astype(BF16),
        "w_down": w_down.astype(BF16),
    }


def _tri(tk):
    r = np.arange(tk)
    return jnp.asarray(r[:, None] > r[None, :], BF16)


def _pad_keys(a, lk):
    return jnp.pad(a, ((0, 0), (0, lk - a.shape[1]), (0, 0)))


def _layer(x, pos, wts, caches, *, tq_dsa, tk_dsa, tq_sb, tk_sb):
    b, l, _ = x.shape
    n = b * l
    tm = min(512, n)
    x2d = x.reshape(n, D_MODEL)
    rows = l if l >= tm else tm
    (ka, va, ki, kb, vb, qa16, ka16, va16, qi16, ki16, qb16, kb16, vb16, wi) = _project(
        x2d, _rope_tables(pos, rows), wts, tm)
    per_seq = lambda a: a.reshape(b, l, a.shape[-1])
    qa16, qi16, qb16, wi = per_seq(qa16), per_seq(qi16), per_seq(qb16), per_seq(wi)
    ka16, va16, ki16, kb16, vb16 = (per_seq(a) for a in (ka16, va16, ki16, kb16, vb16))
    past = 0
    if caches is not None:
        c_ka, c_va, c_ki, c_kb, c_vb = caches
        past = c_ka.shape[1]
        flat = lambda c: c.reshape(b, past, -1).astype(BF16)
        c_ki16 = c_ki.astype(BF16)
        ka16 = jnp.concatenate([flat(c_ka), ka16], axis=1)
        va16 = jnp.concatenate([flat(c_va), va16], axis=1)
        ki16 = jnp.concatenate([jnp.concatenate([c_ki16, c_ki16], axis=2), ki16], axis=1)
        kb16 = jnp.concatenate([flat(c_kb), kb16], axis=1)
        vb16 = jnp.concatenate([flat(c_vb), vb16], axis=1)
    n_valid = past + l
    lk_dsa = pl.cdiv(n_valid, tk_dsa) * tk_dsa
    lk_sb = pl.cdiv(n_valid, tk_sb) * tk_sb
    oa16 = _dsa(wts["logit_bound"], qa16, qi16, wi, _pad_keys(ki16, lk_dsa), _pad_keys(ka16, lk_dsa),
                _pad_keys(va16, lk_dsa), tq=tq_dsa, tk=tk_dsa, n_valid=n_valid, q_off=past)
    ob16 = _sb(qb16, _pad_keys(kb16, lk_sb), _pad_keys(vb16, lk_sb), _tri(tk_sb),
               tq=tq_sb, tk=tk_sb, q_off=past)
    y = _merge_ffn(x2d, oa16.reshape(n, WIDTH), ob16.reshape(n, WIDTH), wts, tm)
    heads = lambda a: a.reshape(b, l, N_HEADS, HEAD_DIM)
    return (y.reshape(b, l, D_MODEL), heads(ka), heads(va), ki.reshape(b, l, IDX_DIM),
            heads(kb), heads(vb))


def kernel(x_prompt, x_sample, cache_k_a, cache_v_a, cache_k_idx, cache_k_sb, cache_v_sb,
           g_mix, w_in, g_qn, g_kn, w_branch_a, w_branch_b, w_out, g_ffn, w_up, w_down):
    wts = _prep_weights(g_mix, w_in, g_qn, g_kn, w_branch_a, w_branch_b, w_out, g_ffn, w_up, w_down)
    s = x_prompt.shape[1]
    t = x_sample.shape[1]
    past = cache_k_a.shape[1]
    prompt = _layer(x_prompt, jnp.arange(s, dtype=jnp.int32), wts, None,
                    tq_dsa=min(256, s), tk_dsa=min(512, s), tq_sb=min(256, s), tk_sb=min(256, s))
    sample = _layer(x_sample, past + jnp.arange(t, dtype=jnp.int32), wts,
                    (cache_k_a, cache_v_a, cache_k_idx, cache_k_sb, cache_v_sb),
                    tq_dsa=t, tk_dsa=512, tq_sb=t, tk_sb=128)
    return (prompt[0], sample[0]) + prompt[1:] + sample[1:]
```

```python
import functools

import numpy as np
import jax
import jax.numpy as jnp
from jax import lax
from jax.experimental import pallas as pl
from jax.experimental.pallas import tpu as pltpu

D_MODEL = 1024
HEAD_DIM = 64
N_HEADS = 8
WIDTH = N_HEADS * HEAD_DIM
IDX_DIM = 64
CHUNK = 64
CHUNK_SHIFT = CHUNK.bit_length() - 1
assert 1 << CHUNK_SHIFT == CHUNK
TOPK_MAX = 256
D_FF = 4 * D_MODEL
ROPE_THETA = 10000.0
EPS = 1e-6
SPLIT_SIZES = (WIDTH, WIDTH, WIDTH, N_HEADS * IDX_DIM, IDX_DIM, N_HEADS,
               WIDTH, WIDTH, WIDTH, D_MODEL, D_MODEL)

LANES = 128
PAIR = 2 * HEAD_DIM
N_PAIRS = N_HEADS // 2
QK_SCALE = HEAD_DIM ** -0.5
W_IDX_SCALE = (N_HEADS ** -0.5) * (IDX_DIM ** -0.5)
INT_MIN = -2 ** 31
NEG_BIG = -1e30
SOFTMAX_SAFE_BOUND = 40.0
SB_EXIT_TAIL = 105.0
VMEM_LIMIT = 56 * 1024 * 1024

F32 = jnp.float32
BF16 = jnp.bfloat16
_NT = (((1,), (1,)), ((), ()))


def _resident(shape, index_map):
    return pl.BlockSpec(shape, index_map, pipeline_mode=pl.Buffered(1))


def _rope(x, cs, sn):
    w = x.shape[1]
    lane = lax.broadcasted_iota(jnp.int32, x.shape, 1)
    first_half = (lane & (HEAD_DIM // 2)) == 0
    partner = jnp.where(first_half, pltpu.roll(x, w - HEAD_DIM // 2, 1),
                        pltpu.roll(x, HEAD_DIM // 2, 1))
    reps = w // LANES
    if reps > 1:
        cs = jnp.concatenate([cs] * reps, axis=1)
        sn = jnp.concatenate([sn] * reps, axis=1)
    return x * cs + partner * sn


def _head_rmsnorm(z, grp, g):
    zz = z * z
    hi = zz.astype(BF16)
    lo = (zz - hi.astype(F32)).astype(BF16)
    ss = (jnp.dot(hi, grp, preferred_element_type=F32)
          + jnp.dot(lo, grp, preferred_element_type=F32))
    return z * lax.rsqrt(ss * (1.0 / HEAD_DIM) + EPS) * g


def _proj_kernel(x_ref, gmix_ref, w_ref, gqn_ref, gkn_ref, cs_ref, sn_ref, grp_ref,
                 ka_ref, va_ref, ki_ref, kb_ref, vb_ref,
                 qa16_ref, ka16_ref, va16_ref, qi16_ref, ki16_ref,
                 qb16_ref, kb16_ref, vb16_ref, wi_ref):
    x = x_ref[...]
    h = (x * lax.rsqrt(jnp.mean(x * x, axis=-1, keepdims=True) + EPS)
         * gmix_ref[...]).astype(BF16)
    cs = cs_ref[...]
    sn = sn_ref[...]
    grp = grp_ref[...]

    def seg(k, width=WIDTH):
        return jnp.dot(h, w_ref[:, k * WIDTH:k * WIDTH + width],
                       preferred_element_type=F32)

    qa = _rope(_head_rmsnorm(seg(0), grp, gqn_ref[...]), cs, sn)
    qa16_ref[...] = (qa * QK_SCALE).astype(BF16)
    ka = _rope(_head_rmsnorm(seg(1), grp, gkn_ref[...]), cs, sn)
    ka_ref[...] = ka
    ka16_ref[...] = ka.astype(BF16)
    va = seg(2)
    va_ref[...] = va
    va16_ref[...] = va.astype(BF16)
    qi16_ref[...] = _rope(seg(3), cs, sn).astype(BF16)
    qb16_ref[...] = (seg(4) * QK_SCALE).astype(BF16)
    kb = seg(5)
    kb_ref[...] = kb
    kb16_ref[...] = kb.astype(BF16)
    vb = seg(6)
    vb_ref[...] = vb
    vb16_ref[...] = vb.astype(BF16)
    misc = seg(7, LANES)
    ki = _rope(misc, cs, sn)[:, :IDX_DIM]
    ki_ref[...] = ki
    ki16 = ki.astype(BF16)
    ki16_ref[...] = jnp.concatenate([ki16, ki16], axis=1)
    wi_ref[...] = misc * W_IDX_SCALE


def _project(x2d, tabs, wts, tm):
    n = x2d.shape[0]
    cs, sn = tabs
    nt = cs.shape[0] // tm
    row = lambda i: (i, 0)
    const = lambda i: (0, 0)
    tab = lambda i: (i % nt, 0)
    f32_leaf = lambda w: jax.ShapeDtypeStruct((n, w), F32)
    b16 = lambda w: jax.ShapeDtypeStruct((n, w), BF16)
    out_shape = [f32_leaf(WIDTH), f32_leaf(WIDTH), f32_leaf(IDX_DIM), f32_leaf(WIDTH), f32_leaf(WIDTH),
                 b16(WIDTH), b16(WIDTH), b16(WIDTH), b16(WIDTH), b16(LANES),
                 b16(WIDTH), b16(WIDTH), b16(WIDTH), f32_leaf(LANES)]
    out_specs = [pl.BlockSpec((tm, s.shape[1]), row) for s in out_shape]
    w_proj = wts["w_proj"]
    return pl.pallas_call(
        _proj_kernel,
        grid=(n // tm,),
        in_specs=[pl.BlockSpec((tm, D_MODEL), row),
                  pl.BlockSpec((1, D_MODEL), const),
                  _resident(w_proj.shape, const),
                  pl.BlockSpec((1, WIDTH), const),
                  pl.BlockSpec((1, WIDTH), const),
                  pl.BlockSpec((tm, LANES), tab),
                  pl.BlockSpec((tm, LANES), tab),
                  _resident((WIDTH, WIDTH), const)],
        out_specs=out_specs,
        out_shape=out_shape,
        compiler_params=pltpu.CompilerParams(dimension_semantics=("arbitrary",),
                                             vmem_limit_bytes=VMEM_LIMIT),
        name="proj",
    )(x2d, wts["g_mix"], w_proj, wts["g_qn"], wts["g_kn"], cs, sn, wts["grp"])


SUBLANES = 8
COUNT_CHAINS = 4


def _upper_half(x):
    hi = lax.bitcast_convert_type(x, jnp.int32) & jnp.int32(-65536)
    return lax.bitcast_convert_type(hi, F32).astype(BF16)


def _key_to_float(k):
    return lax.bitcast_convert_type(k ^ ((k >> 31) & 0x7FFFFFFF), F32)


def _mask_heads(src_ref, dst_ref):
    tq = src_ref.shape[0]
    lane = lax.broadcasted_iota(jnp.int32, (tq, PAIR), 1)
    for p in range(N_PAIRS):
        pair = src_ref[:, p * PAIR:(p + 1) * PAIR]
        zero = jnp.zeros_like(pair)
        dst_ref[p, :tq, :] = jnp.where(lane < HEAD_DIM, pair, zero)
        dst_ref[p, tq:, :] = jnp.where(lane >= HEAD_DIM, pair, zero)


def _lanes(x, reps):
    return jnp.concatenate([x] * reps, axis=1) if reps > 1 else x


def _dsa_kernel(bound_ref, qa_ref, qi_ref, wi_ref, ki_ref, ka_ref, va_ref, o_ref,
                sc_ref, sc16_ref, qam_ref, qim_ref, w_ref, shift_ref, l_ref, acc_ref,
                thr_ref, cut_ref, *, tq, tk, n_valid, q_off, topk):
    i = pl.program_id(1)
    q0 = q_off + i * tq
    qpos = q0 + lax.broadcasted_iota(jnp.int32, (1, tq), 1)
    chunk_end = lambda pos: ((pos >> CHUNK_SHIFT) + 1) << CHUNK_SHIFT
    lim = jnp.minimum(chunk_end(qpos), n_valid)
    n_adm = jnp.minimum(chunk_end(q0 + tq - 1), n_valid)
    nkb = (n_adm + tk - 1) // tk
    reps = tk // LANES
    key_iota = lax.broadcasted_iota(jnp.int32, (tk, 1), 0)

    _mask_heads(qi_ref, qim_ref)
    _mask_heads(qa_ref, qam_ref)
    w_t = wi_ref[...].T
    for p in range(N_PAIRS):
        r0 = IDX_DIM + 2 * p
        row = jnp.concatenate([w_t[r0:r0 + 1], w_t[r0 + 1:r0 + 2]], axis=1)
        w_ref[p] = jnp.broadcast_to(row, (SUBLANES, 2 * tq))

    def score_block(j, carry):
        ks = pl.multiple_of(j * tk, tk)
        kib = ki_ref[pl.ds(ks, tk), :]
        score = jnp.zeros((tk, tq), F32)
        for p in range(N_PAIRS):
            s = lax.dot_general(kib, qim_ref[p], _NT, preferred_element_type=F32)
            r = jnp.maximum(s, 0.0) * w_ref[p][:1]
            score = score + r[:, :tq] + r[:, tq:]
        score = jnp.where(ks + key_iota < lim, score, -jnp.inf)
        sc_ref[pl.ds(ks, tk), :] = score
        sc16_ref[pl.ds(ks, tk), :] = _upper_half(score)
        return carry

    lax.fori_loop(0, nkb, score_block, 0)

    def count(src_ref, pred):
        dt = src_ref.dtype
        sub = SUBLANES * 4 // jnp.dtype(dt).itemsize
        one, zero = jnp.ones((), dt), jnp.zeros((), dt)

        def body(c, acc):
            ks = pl.multiple_of(c * tk, tk)
            blk = src_ref[pl.ds(ks, tk), :]
            parts = [jnp.zeros((sub, tq), dt) for _ in range(COUNT_CHAINS)]
            for r in range(tk // sub):
                hit = pred(blk[r * sub:(r + 1) * sub], ks + r * sub)
                parts[r % COUNT_CHAINS] = parts[r % COUNT_CHAINS] + jnp.where(hit, one, zero)
            part = functools.reduce(lambda a, b: a + b, parts).astype(F32)
            for half in range(sub // SUBLANES - 1):
                part = part[:SUBLANES] + part[SUBLANES:]
            return acc + part

        acc = lax.fori_loop(0, nkb, body, jnp.zeros((SUBLANES, tq), F32))
        return jnp.broadcast_to(jnp.sum(acc, axis=0, keepdims=True), (SUBLANES, tq))

    def upper_rows(x):
        u = _upper_half(x)
        return jnp.concatenate([u] * (4 // jnp.dtype(u.dtype).itemsize), axis=0)

    kf = float(topk)
    nonneg = count(sc16_ref, lambda blk, k0: blk >= 0.0) >= kf
    key = jnp.where(nonneg, jnp.zeros((SUBLANES, tq), jnp.int32),
                    jnp.full((SUBLANES, tq), INT_MIN, jnp.int32))

    def bit_step_upper(t, key):
        cand = key | (jnp.int32(1) << (30 - t))
        cand16 = upper_rows(_key_to_float(cand))
        return jnp.where(count(sc16_ref, lambda blk, k0: blk >= cand16) >= kf, cand, key)

    def bit_step(t, key):
        cand = key | (jnp.int32(1) << (30 - t))
        cand_f = _key_to_float(cand)
        return jnp.where(count(sc_ref, lambda blk, k0: blk >= cand_f) >= kf, cand, key)

    key = lax.fori_loop(0, 15, bit_step_upper, key)
    key = lax.fori_loop(15, 31, bit_step, key)
    has_thr = key != INT_MIN
    thr = jnp.where(has_thr, _key_to_float(key), -jnp.inf)
    thr_ref[...] = thr
    need = kf - count(sc_ref, lambda blk, k0: blk > thr)
    c_eq = count(sc_ref, lambda blk, k0: blk == thr)
    cut_ref[...] = jnp.where(has_thr, jnp.int32(2 ** 30), jnp.int32(-1))
    split = has_thr & (c_eq > need)

    @pl.when(jnp.max(jnp.where(split, 1.0, 0.0)) > 0.0)
    def _():
        sub_iota = lax.broadcasted_iota(jnp.int32, (SUBLANES, 1), 0)

        def idx_step(t, cut):
            cand = cut | (jnp.int32(1) << (13 - t))
            below = count(sc_ref, lambda blk, k0: (blk == thr) & (k0 + sub_iota < cand))
            return jnp.where(below <= need - 1.0, cand, cut)

        cut = lax.fori_loop(0, 14, idx_step, jnp.zeros((SUBLANES, tq), jnp.int32))
        cut_ref[...] = jnp.where(has_thr, cut, -1)

    def masked_bias(ks):
        scb = sc_ref[pl.ds(ks, tk), :]
        thr_b = thr_ref[:1, :]
        sel = (scb > thr_b) | ((scb == thr_b) & (ks + key_iota <= cut_ref[:1, :]))
        bias = jnp.where(sel, 0.0, NEG_BIG).T
        return jnp.concatenate([bias, bias], axis=0)

    def logits(p, kab):
        return lax.dot_general(qam_ref[p], kab[:, p * PAIR:(p + 1) * PAIR], _NT,
                               preferred_element_type=F32)

    bound = bound_ref[0]
    shift_ref[...] = jnp.full(shift_ref.shape, bound, F32)

    @pl.when(bound > SOFTMAX_SAFE_BOUND)
    def _():
        l_ref[...] = jnp.full(l_ref.shape, NEG_BIG, F32)

        def max_block(j, carry):
            ks = pl.multiple_of(j * tk, tk)
            bias = masked_bias(ks)
            kab = ka_ref[pl.ds(ks, tk), :]
            for p in range(N_PAIRS):
                s = logits(p, kab) + bias
                mx = l_ref[p]
                for u in range(reps):
                    mx = jnp.maximum(mx, s[:, u * LANES:(u + 1) * LANES])
                l_ref[p] = mx
            return carry

        lax.fori_loop(0, nkb, max_block, 0)
        for p in range(N_PAIRS):
            shift_ref[p] = jnp.broadcast_to(jnp.max(l_ref[p], axis=1, keepdims=True),
                                            (2 * tq, LANES))

    l_ref[...] = jnp.zeros(l_ref.shape, F32)
    acc_ref[...] = jnp.zeros(acc_ref.shape, F32)

    def attend_block(j, carry):
        ks = pl.multiple_of(j * tk, tk)
        bias = masked_bias(ks)
        kab = ka_ref[pl.ds(ks, tk), :]
        vab = va_ref[pl.ds(ks, tk), :]
        for p in range(N_PAIRS):
            e = jnp.exp((logits(p, kab) - _lanes(shift_ref[p], reps)) + bias)
            part = l_ref[p]
            for u in range(reps):
                part = part + e[:, u * LANES:(u + 1) * LANES]
            l_ref[p] = part
            acc_ref[p] = acc_ref[p] + jnp.dot(
                e.astype(BF16), vab[:, p * PAIR:(p + 1) * PAIR], preferred_element_type=F32)
        return carry

    lax.fori_loop(0, nkb, attend_block, 0)

    lane_o = lax.broadcasted_iota(jnp.int32, (tq, PAIR), 1)
    for p in range(N_PAIRS):
        out = acc_ref[p] / jnp.sum(l_ref[p], axis=1, keepdims=True)
        o_ref[:, p * PAIR:(p + 1) * PAIR] = jnp.where(
            lane_o < HEAD_DIM, out[:tq], out[tq:]).astype(BF16)


def _dsa(bound, qa16, qi16, wi, ki16, ka16, va16, *, tq, tk, n_valid, q_off):
    b, lq, _ = qa16.shape
    lk = ka16.shape[1]
    topk = min(TOPK_MAX, n_valid // 4)
    lq_pad = pl.cdiv(lq, LANES) * LANES
    if lq_pad != lq:
        pad = lambda a: jnp.pad(a, ((0, 0), (0, lq_pad - lq), (0, 0)))
        qa16, qi16, wi = pad(qa16), pad(qi16), pad(wi)
    tq = min(tq, lq_pad)
    assert tq % LANES == 0 and lq_pad % tq == 0 and lk % tk == 0
    qblk = lambda bb, i: (bb, i, 0)
    kblk = lambda bb, i: (bb, 0, 0)
    kern = functools.partial(_dsa_kernel, tq=tq, tk=tk, n_valid=n_valid, q_off=q_off, topk=topk)
    stacked = lambda w, dt: pltpu.VMEM((N_PAIRS, 2 * tq, w), dt)
    out = pl.pallas_call(
        kern,
        grid=(b, lq_pad // tq),
        in_specs=[pl.BlockSpec(memory_space=pltpu.SMEM),
                  pl.BlockSpec((None, tq, WIDTH), qblk),
                  pl.BlockSpec((None, tq, WIDTH), qblk),
                  pl.BlockSpec((None, tq, LANES), qblk),
                  _resident((None, lk, LANES), kblk),
                  _resident((None, lk, WIDTH), kblk),
                  _resident((None, lk, WIDTH), kblk)],
        out_specs=pl.BlockSpec((None, tq, WIDTH), qblk),
        out_shape=jax.ShapeDtypeStruct((b, lq_pad, WIDTH), BF16),
        scratch_shapes=[pltpu.VMEM((lk, tq), F32),
                        pltpu.VMEM((lk, tq), BF16),
                        stacked(PAIR, BF16),
                        stacked(PAIR, BF16),
                        pltpu.VMEM((N_PAIRS, SUBLANES, 2 * tq), F32),
                        stacked(LANES, F32),
                        stacked(LANES, F32),
                        stacked(PAIR, F32),
                        pltpu.VMEM((SUBLANES, tq), F32),
                        pltpu.VMEM((SUBLANES, tq), jnp.int32)],
        compiler_params=pltpu.CompilerParams(dimension_semantics=("arbitrary", "arbitrary"),
                                             vmem_limit_bytes=VMEM_LIMIT),
        name="dsa",
    )(bound, qa16, qi16, wi, ki16, ka16, va16)
    return out[:, :lq]


def _sb_kernel(qb_ref, kb_ref, vb_ref, tri_ref, o_ref, qbm_ref, tail_ref, acc_ref,
               *, tq, tk, q_off):
    i = pl.program_id(1)
    q0 = q_off + i * tq
    qpos = q0 + lax.broadcasted_iota(jnp.int32, (tq, 1), 0)
    qpos = jnp.concatenate([qpos, qpos], axis=0)
    lane_kpos = lax.broadcasted_iota(jnp.int32, (1, tk), 1)
    tri = tri_ref[...]

    _mask_heads(qb_ref, qbm_ref)
    tail_ref[...] = jnp.zeros(tail_ref.shape, F32)
    acc_ref[...] = jnp.zeros(acc_ref.shape, F32)

    def block(state):
        j, _ = state
        ks = pl.multiple_of(j * tk, tk)
        causal = (ks + lane_kpos) < qpos
        kbb = kb_ref[pl.ds(ks, tk), :]
        vbb = vb_ref[pl.ds(ks, tk), :]
        tail_min = jnp.full((2 * tq, 1), jnp.inf, F32)
        for p in range(N_PAIRS):
            cols = slice(p * PAIR, (p + 1) * PAIR)
            z = lax.dot_general(qbm_ref[p], kbb[:, cols], _NT, preferred_element_type=F32)
            lp = jnp.log(1.0 + jnp.exp(-jnp.abs(z)))
            sp = jnp.where(causal, jnp.maximum(z, 0.0) + lp, 0.0)
            hi = sp.astype(BF16)
            lo = (sp - hi.astype(F32)).astype(BF16)
            inner = (jnp.dot(hi, tri, preferred_element_type=F32)
                     + jnp.dot(lo, tri, preferred_element_type=F32))
            tail_old = tail_ref[p]
            log_a = (jnp.minimum(z, 0.0) - lp) - inner - tail_old[:, :1]
            a = jnp.where(causal, jnp.exp(log_a), 0.0)
            acc_ref[p] = acc_ref[p] + jnp.dot(a.astype(BF16), vbb[:, cols],
                                              preferred_element_type=F32)
            tail_new = tail_old + jnp.sum(sp, axis=1, keepdims=True)
            tail_ref[p] = tail_new
            tail_min = jnp.minimum(tail_min, tail_new[:, :1])
        return j - 1, jnp.min(tail_min)

    def more(state):
        j, tail_min = state
        return (j >= 0) & (tail_min <= SB_EXIT_TAIL)

    lax.while_loop(more, block, ((q0 + tq - 1) // tk, jnp.float32(0.0)))

    lane = lax.broadcasted_iota(jnp.int32, (tq, PAIR), 1)
    for p in range(N_PAIRS):
        acc = acc_ref[p]
        o_ref[:, p * PAIR:(p + 1) * PAIR] = jnp.where(
            lane < HEAD_DIM, acc[:tq], acc[tq:]).astype(BF16)


def _sb(qb16, kb16, vb16, tri, *, tq, tk, q_off):
    b, lq, _ = qb16.shape
    lk = kb16.shape[1]
    qblk = lambda bb, i: (bb, i, 0)
    kblk = lambda bb, i: (bb, 0, 0)
    kern = functools.partial(_sb_kernel, tq=tq, tk=tk, q_off=q_off)
    return pl.pallas_call(
        kern,
        grid=(b, lq // tq),
        in_specs=[pl.BlockSpec((None, tq, WIDTH), qblk),
                  _resident((None, lk, WIDTH), kblk),
                  _resident((None, lk, WIDTH), kblk),
                  _resident((tk, tk), lambda bb, i: (0, 0))],
        out_specs=pl.BlockSpec((None, tq, WIDTH), qblk),
        out_shape=jax.ShapeDtypeStruct((b, lq, WIDTH), BF16),
        scratch_shapes=[pltpu.VMEM((N_PAIRS, 2 * tq, PAIR), BF16),
                        pltpu.VMEM((N_PAIRS, 2 * tq, LANES), F32),
                        pltpu.VMEM((N_PAIRS, 2 * tq, PAIR), F32)],
        compiler_params=pltpu.CompilerParams(dimension_semantics=("arbitrary", "arbitrary"),
                                             vmem_limit_bytes=VMEM_LIMIT),
        name="sb",
    )(qb16, kb16, vb16, tri)


FF_CHUNK = 1024


def _rms_rows(x, g):
    return x * lax.rsqrt(jnp.mean(x * x, axis=-1, keepdims=True) + EPS) * g


def _merge_kernel(x_ref, oa_ref, ob_ref, gmix_ref, gffn_ref, wg_ref, wba_ref, wbb_ref,
                  wout_ref, wup_ref, wdown_ref, y_ref):
    x = x_ref[...]
    hx = _rms_rows(x, gmix_ref[...]).astype(BF16)
    gate_a = jnp.dot(hx, wg_ref[:, :D_MODEL], preferred_element_type=F32)
    gate_b = jnp.dot(hx, wg_ref[:, D_MODEL:], preferred_element_type=F32)
    pa = jnp.dot(oa_ref[...], wba_ref[...], preferred_element_type=F32)
    pb = jnp.dot(ob_ref[...], wbb_ref[...], preferred_element_type=F32)
    m = jax.nn.sigmoid(gate_a) * pa + jax.nn.sigmoid(gate_b) * pb
    h = x + jnp.dot(m.astype(BF16), wout_ref[...], preferred_element_type=F32)
    hn = _rms_rows(h, gffn_ref[...]).astype(BF16)
    y = h
    for c in range(D_FF // FF_CHUNK):
        u = jnp.dot(hn, wup_ref[:, c * FF_CHUNK:(c + 1) * FF_CHUNK], preferred_element_type=F32)
        r = jnp.square(jnp.maximum(u, 0.0)).astype(BF16)
        y = y + jnp.dot(r, wdown_ref[c * FF_CHUNK:(c + 1) * FF_CHUNK, :],
                        preferred_element_type=F32)
    y_ref[...] = y


def _merge_ffn(x2d, oa16, ob16, wts, tm):
    n = x2d.shape[0]
    row = lambda i: (i, 0)
    const = lambda i: (0, 0)
    weights = [wts["w_gate"], wts["w_branch_a"], wts["w_branch_b"], wts["w_out"],
               wts["w_up"], wts["w_down"]]
    return pl.pallas_call(
        _merge_kernel,
        grid=(n // tm,),
        in_specs=[pl.BlockSpec((tm, D_MODEL), row),
                  pl.BlockSpec((tm, WIDTH), row),
                  pl.BlockSpec((tm, WIDTH), row),
                  pl.BlockSpec((1, D_MODEL), const),
                  pl.BlockSpec((1, D_MODEL), const)]
                 + [_resident(w.shape, const) for w in weights],
        out_specs=pl.BlockSpec((tm, D_MODEL), row),
        out_shape=jax.ShapeDtypeStruct((n, D_MODEL), F32),
        compiler_params=pltpu.CompilerParams(dimension_semantics=("arbitrary",),
                                             vmem_limit_bytes=VMEM_LIMIT),
        name="merge_ffn",
    )(x2d, oa16, ob16, wts["g_mix"], wts["g_ffn"], *weights)


def _rope_tables(pos, rows):
    half = HEAD_DIM // 2
    inv_freq = jnp.power(ROPE_THETA, -jnp.arange(half, dtype=F32) / half)
    ang = pos.astype(F32)[:, None] * inv_freq[None, :]
    cos, sin = jnp.cos(ang), jnp.sin(ang)
    cs = jnp.concatenate([cos, cos, cos, cos], axis=1)
    sn = jnp.concatenate([-sin, sin, -sin, sin], axis=1)
    reps = rows // pos.shape[0]
    if reps > 1:
        cs, sn = jnp.tile(cs, (reps, 1)), jnp.tile(sn, (reps, 1))
    return cs, sn


def _prep_weights(g_mix, w_in, g_qn, g_kn, w_branch_a, w_branch_b, w_out, g_ffn, w_up, w_down):
    offs = [int(v) for v in np.cumsum(SPLIT_SIZES)[:-1]]
    w_qa, w_ka, w_va, w_qi, w_ki, w_wi, w_qb, w_kb, w_vb, w_ga, w_gb = jnp.split(w_in, offs, axis=1)
    pad = jnp.zeros((D_MODEL, LANES - IDX_DIM - N_HEADS), w_in.dtype)
    w_proj = jnp.concatenate([w_qa, w_ka, w_va, w_qi, w_qb, w_kb, w_vb, w_ki, w_wi, pad], axis=1)
    head = np.arange(WIDTH) // HEAD_DIM
    return {
        "w_proj": w_proj.astype(BF16),
        "w_gate": jnp.concatenate([w_ga, w_gb], axis=1).astype(BF16),
        "grp": jnp.asarray(head[:, None] == head[None, :], BF16),
        "g_mix": g_mix.reshape(1, D_MODEL),
        "g_ffn": g_ffn.reshape(1, D_MODEL),
        "g_qn": jnp.tile(g_qn, N_HEADS).reshape(1, WIDTH),
        "g_kn": jnp.tile(g_kn, N_HEADS).reshape(1, WIDTH),
        "logit_bound": (HEAD_DIM * QK_SCALE * jnp.max(jnp.abs(g_qn))
                        * jnp.max(jnp.abs(g_kn))).reshape(1).astype(F32),
        "w_branch_a": w_branch_a.astype(BF16),
        "w_branch_b": w_branch_b.astype(BF16),
        "w_out": w_out.astype(BF16),
        "w_up": w_up.astype(BF16),
        "w_down": w_down.astype(BF16),
    }


def _tri(tk):
    r = np.arange(tk)
    return jnp.asarray(r[:, None] > r[None, :], BF16)


def _pad_keys(a, lk):
    return jnp.pad(a, ((0, 0), (0, lk - a.shape[1]), (0, 0)))


def _layer(x, pos, wts, caches, *, tq_dsa, tk_dsa, tq_sb, tk_sb):
    b, l, _ = x.shape
    n = b * l
    tm = min(512, n)
    x2d = x.reshape(n, D_MODEL)
    rows = l if l >= tm else tm
    (ka, va, ki, kb, vb, qa16, ka16, va16, qi16, ki16, qb16, kb16, vb16, wi) = _project(
        x2d, _rope_tables(pos, rows), wts, tm)
    per_seq = lambda a: a.reshape(b, l, a.shape[-1])
    qa16, qi16, qb16, wi = per_seq(qa16), per_seq(qi16), per_seq(qb16), per_seq(wi)
    ka16, va16, ki16, kb16, vb16 = (per_seq(a) for a in (ka16, va16, ki16, kb16, vb16))
    past = 0
    if caches is not None:
        c_ka, c_va, c_ki, c_kb, c_vb = caches
        past = c_ka.shape[1]
        flat = lambda c: c.reshape(b, past, -1).astype(BF16)
        c_ki16 = c_ki.astype(BF16)
        ka16 = jnp.concatenate([flat(c_ka), ka16], axis=1)
        va16 = jnp.concatenate([flat(c_va), va16], axis=1)
        ki16 = jnp.concatenate([jnp.concatenate([c_ki16, c_ki16], axis=2), ki16], axis=1)
        kb16 = jnp.concatenate([flat(c_kb), kb16], axis=1)
        vb16 = jnp.concatenate([flat(c_vb), vb16], axis=1)
    n_valid = past + l
    lk_dsa = pl.cdiv(n_valid, tk_dsa) * tk_dsa
    lk_sb = pl.cdiv(n_valid, tk_sb) * tk_sb
    oa16 = _dsa(wts["logit_bound"], qa16, qi16, wi, _pad_keys(ki16, lk_dsa), _pad_keys(ka16, lk_dsa),
                _pad_keys(va16, lk_dsa), tq=tq_dsa, tk=tk_dsa, n_valid=n_valid, q_off=past)
    ob16 = _sb(qb16, _pad_keys(kb16, lk_sb), _pad_keys(vb16, lk_sb), _tri(tk_sb),
               tq=tq_sb, tk=tk_sb, q_off=past)
    y = _merge_ffn(x2d, oa16.reshape(n, WIDTH), ob16.reshape(n, WIDTH), wts, tm)
    heads = lambda a: a.reshape(b, l, N_HEADS, HEAD_DIM)
    return (y.reshape(b, l, D_MODEL), heads(ka), heads(va), ki.reshape(b, l, IDX_DIM),
            heads(kb), heads(vb))


def kernel(x_prompt, x_sample, cache_k_a, cache_v_a, cache_k_idx, cache_k_sb, cache_v_sb,
           g_mix, w_in, g_qn, g_kn, w_branch_a, w_branch_b, w_out, g_ffn, w_up, w_down):
    wts = _prep_weights(g_mix, w_in, g_qn, g_kn, w_branch_a, w_branch_b, w_out, g_ffn, w_up, w_down)
    s = x_prompt.shape[1]
    t = x_sample.shape[1]
    past = cache_k_a.shape[1]
    prompt = _layer(x_prompt, jnp.arange(s, dtype=jnp.int32), wts, None,
                    tq_dsa=min(256, s), tk_dsa=min(512, s), tq_sb=min(256, s), tk_sb=min(256, s))
    sample = _layer(x_sample, past + jnp.arange(t, dtype=jnp.int32), wts,
                    (cache_k_a, cache_v_a, cache_k_idx, cache_k_sb, cache_v_sb),
                    tq_dsa=LANES, tk_dsa=512, tq_sb=t, tk_sb=128)
    return (prompt[0], sample[0]) + prompt[1:] + sample[1:]
```

```python
import functools

import numpy as np
import jax
import jax.numpy as jnp
from jax import lax
from jax.experimental import pallas as pl
from jax.experimental.pallas import tpu as pltpu

D_MODEL = 1024
HEAD_DIM = 64
N_HEADS = 8
WIDTH = N_HEADS * HEAD_DIM
IDX_DIM = 64
CHUNK = 64
CHUNK_SHIFT = CHUNK.bit_length() - 1
assert 1 << CHUNK_SHIFT == CHUNK
TOPK_MAX = 256
D_FF = 4 * D_MODEL
ROPE_THETA = 10000.0
EPS = 1e-6
SPLIT_SIZES = (WIDTH, WIDTH, WIDTH, N_HEADS * IDX_DIM, IDX_DIM, N_HEADS,
               WIDTH, WIDTH, WIDTH, D_MODEL, D_MODEL)

LANES = 128
PAIR = 2 * HEAD_DIM
N_PAIRS = N_HEADS // 2
QK_SCALE = HEAD_DIM ** -0.5
W_IDX_SCALE = (N_HEADS ** -0.5) * (IDX_DIM ** -0.5)
INT_MIN = -2 ** 31
NEG_BIG = -1e30
SOFTMAX_SAFE_BOUND = 40.0
SB_EXIT_TAIL = 105.0
VMEM_LIMIT = 56 * 1024 * 1024

F32 = jnp.float32
BF16 = jnp.bfloat16
_NT = (((1,), (1,)), ((), ()))


def _resident(shape, index_map):
    return pl.BlockSpec(shape, index_map, pipeline_mode=pl.Buffered(1))


def _per_batch(shape, index_map, steps_per_batch):
    if steps_per_batch == 1:
        return pl.BlockSpec(shape, index_map)
    return _resident(shape, index_map)


def _rope(x, cs, sn):
    w = x.shape[1]
    lane = lax.broadcasted_iota(jnp.int32, x.shape, 1)
    first_half = (lane & (HEAD_DIM // 2)) == 0
    partner = jnp.where(first_half, pltpu.roll(x, w - HEAD_DIM // 2, 1),
                        pltpu.roll(x, HEAD_DIM // 2, 1))
    reps = w // LANES
    if reps > 1:
        cs = jnp.concatenate([cs] * reps, axis=1)
        sn = jnp.concatenate([sn] * reps, axis=1)
    return x * cs + partner * sn


def _head_rmsnorm(z, grp, g):
    zz = z * z
    hi = zz.astype(BF16)
    lo = (zz - hi.astype(F32)).astype(BF16)
    ss = (jnp.dot(hi, grp, preferred_element_type=F32)
          + jnp.dot(lo, grp, preferred_element_type=F32))
    return z * lax.rsqrt(ss * (1.0 / HEAD_DIM) + EPS) * g


def _proj_kernel(x_ref, gmix_ref, w_ref, gqn_ref, gkn_ref, cs_ref, sn_ref, grp_ref,
                 ka_ref, va_ref, ki_ref, kb_ref, vb_ref,
                 qa16_ref, ka16_ref, va16_ref, qi16_ref, ki16_ref,
                 qb16_ref, kb16_ref, vb16_ref, wi_ref):
    x = x_ref[...]
    h = (x * lax.rsqrt(jnp.mean(x * x, axis=-1, keepdims=True) + EPS)
         * gmix_ref[...]).astype(BF16)
    cs = cs_ref[...]
    sn = sn_ref[...]
    grp = grp_ref[...]

    def seg(k, width=WIDTH):
        return jnp.dot(h, w_ref[:, k * WIDTH:k * WIDTH + width],
                       preferred_element_type=F32)

    split_heads = lambda z: pltpu.einshape("m(hd)->mhd", z, h=N_HEADS)

    qa = _rope(_head_rmsnorm(seg(0), grp, gqn_ref[...]), cs, sn)
    qa16_ref[...] = (qa * QK_SCALE).astype(BF16)
    ka = _rope(_head_rmsnorm(seg(1), grp, gkn_ref[...]), cs, sn)
    ka_ref[...] = split_heads(ka)
    ka16_ref[...] = ka.astype(BF16)
    va = seg(2)
    va_ref[...] = split_heads(va)
    va16_ref[...] = va.astype(BF16)
    qi16_ref[...] = _rope(seg(3), cs, sn).astype(BF16)
    qb16_ref[...] = (seg(4) * QK_SCALE).astype(BF16)
    kb = seg(5)
    kb_ref[...] = split_heads(kb)
    kb16_ref[...] = kb.astype(BF16)
    vb = seg(6)
    vb_ref[...] = split_heads(vb)
    vb16_ref[...] = vb.astype(BF16)
    misc = seg(7, LANES)
    ki = _rope(misc, cs, sn)[:, :IDX_DIM]
    ki_ref[...] = ki
    ki16 = ki.astype(BF16)
    ki16_ref[...] = jnp.concatenate([ki16, ki16], axis=1)
    wi_ref[...] = misc * W_IDX_SCALE


def _project(x2d, tabs, wts, tm):
    n = x2d.shape[0]
    cs, sn = tabs
    nt = cs.shape[0] // tm
    row = lambda i: (i, 0)
    const = lambda i: (0, 0)
    tab = lambda i: (i % nt, 0)
    f32_leaf = lambda w: jax.ShapeDtypeStruct((n, w), F32)
    heads_leaf = jax.ShapeDtypeStruct((n, N_HEADS, HEAD_DIM), F32)
    b16 = lambda w: jax.ShapeDtypeStruct((n, w), BF16)
    out_shape = [heads_leaf, heads_leaf, f32_leaf(IDX_DIM), heads_leaf, heads_leaf,
                 b16(WIDTH), b16(WIDTH), b16(WIDTH), b16(WIDTH), b16(LANES),
                 b16(WIDTH), b16(WIDTH), b16(WIDTH), f32_leaf(LANES)]
    out_specs = [pl.BlockSpec((tm,) + s.shape[1:], lambda i, nd=len(s.shape): (i,) + (0,) * (nd - 1))
                 for s in out_shape]
    w_proj = wts["w_proj"]
    return pl.pallas_call(
        _proj_kernel,
        grid=(n // tm,),
        in_specs=[pl.BlockSpec((tm, D_MODEL), row),
                  pl.BlockSpec((1, D_MODEL), const),
                  _resident(w_proj.shape, const),
                  pl.BlockSpec((1, WIDTH), const),
                  pl.BlockSpec((1, WIDTH), const),
                  pl.BlockSpec((tm, LANES), tab),
                  pl.BlockSpec((tm, LANES), tab),
                  _resident((WIDTH, WIDTH), const)],
        out_specs=out_specs,
        out_shape=out_shape,
        compiler_params=pltpu.CompilerParams(dimension_semantics=("arbitrary",),
                                             vmem_limit_bytes=VMEM_LIMIT),
        name="proj",
    )(x2d, wts["g_mix"], w_proj, wts["g_qn"], wts["g_kn"], cs, sn, wts["grp"])


SUBLANES = 8
COUNT_CHAINS = 4


def _upper_half(x):
    hi = lax.bitcast_convert_type(x, jnp.int32) & jnp.int32(-65536)
    return lax.bitcast_convert_type(hi, F32).astype(BF16)


def _key_to_float(k):
    return lax.bitcast_convert_type(k ^ ((k >> 31) & 0x7FFFFFFF), F32)


def _mask_heads(src_ref, dst_ref):
    tq = src_ref.shape[0]
    lane = lax.broadcasted_iota(jnp.int32, (tq, PAIR), 1)
    for p in range(N_PAIRS):
        pair = src_ref[:, p * PAIR:(p + 1) * PAIR]
        zero = jnp.zeros_like(pair)
        dst_ref[p, :tq, :] = jnp.where(lane < HEAD_DIM, pair, zero)
        dst_ref[p, tq:, :] = jnp.where(lane >= HEAD_DIM, pair, zero)


def _lanes(x, reps):
    return jnp.concatenate([x] * reps, axis=1) if reps > 1 else x


def _dsa_kernel(bound_ref, qa_ref, qi_ref, wi_ref, ki_ref, ka_ref, va_ref, o_ref,
                sc_ref, sc16_ref, qam_ref, qim_ref, w_ref, shift_ref, l_ref, acc_ref,
                thr_ref, cut_ref, *, tq, tk, n_valid, q_off, topk):
    i = pl.program_id(1)
    q0 = q_off + i * tq
    qpos = q0 + lax.broadcasted_iota(jnp.int32, (1, tq), 1)
    chunk_end = lambda pos: ((pos >> CHUNK_SHIFT) + 1) << CHUNK_SHIFT
    lim = jnp.minimum(chunk_end(qpos), n_valid)
    n_adm = jnp.minimum(chunk_end(q0 + tq - 1), n_valid)
    nkb = (n_adm + tk - 1) // tk
    reps = tk // LANES
    key_iota = lax.broadcasted_iota(jnp.int32, (tk, 1), 0)

    _mask_heads(qi_ref, qim_ref)
    _mask_heads(qa_ref, qam_ref)
    w_t = wi_ref[...].T
    for p in range(N_PAIRS):
        r0 = IDX_DIM + 2 * p
        row = jnp.concatenate([w_t[r0:r0 + 1], w_t[r0 + 1:r0 + 2]], axis=1)
        w_ref[p] = jnp.broadcast_to(row, (SUBLANES, 2 * tq))

    def score_block(j, carry):
        ks = pl.multiple_of(j * tk, tk)
        kib = ki_ref[pl.ds(ks, tk), :]
        score = jnp.zeros((tk, tq), F32)
        for p in range(N_PAIRS):
            s = lax.dot_general(kib, qim_ref[p], _NT, preferred_element_type=F32)
            r = jnp.maximum(s, 0.0) * w_ref[p][:1]
            score = score + r[:, :tq] + r[:, tq:]
        score = jnp.where(ks + key_iota < lim, score, -jnp.inf)
        sc_ref[pl.ds(ks, tk), :] = score
        sc16_ref[pl.ds(ks, tk), :] = _upper_half(score)
        return carry

    lax.fori_loop(0, nkb, score_block, 0)

    def count(src_ref, pred):
        dt = src_ref.dtype
        sub = SUBLANES * 4 // jnp.dtype(dt).itemsize
        one, zero = jnp.ones((), dt), jnp.zeros((), dt)

        def body(c, acc):
            ks = pl.multiple_of(c * tk, tk)
            blk = src_ref[pl.ds(ks, tk), :]
            parts = [jnp.zeros((sub, tq), dt) for _ in range(COUNT_CHAINS)]
            for r in range(tk // sub):
                hit = pred(blk[r * sub:(r + 1) * sub], ks + r * sub)
                parts[r % COUNT_CHAINS] = parts[r % COUNT_CHAINS] + jnp.where(hit, one, zero)
            part = functools.reduce(lambda a, b: a + b, parts).astype(F32)
            for half in range(sub // SUBLANES - 1):
                part = part[:SUBLANES] + part[SUBLANES:]
            return acc + part

        acc = lax.fori_loop(0, nkb, body, jnp.zeros((SUBLANES, tq), F32))
        return jnp.broadcast_to(jnp.sum(acc, axis=0, keepdims=True), (SUBLANES, tq))

    def upper_rows(x):
        u = _upper_half(x)
        return jnp.concatenate([u] * (4 // jnp.dtype(u.dtype).itemsize), axis=0)

    kf = float(topk)
    nonneg = count(sc16_ref, lambda blk, k0: blk >= 0.0) >= kf
    key = jnp.where(nonneg, jnp.zeros((SUBLANES, tq), jnp.int32),
                    jnp.full((SUBLANES, tq), INT_MIN, jnp.int32))

    def bit_step_upper(t, key):
        cand = key | (jnp.int32(1) << (30 - t))
        cand16 = upper_rows(_key_to_float(cand))
        return jnp.where(count(sc16_ref, lambda blk, k0: blk >= cand16) >= kf, cand, key)

    def bit_step(t, key):
        cand = key | (jnp.int32(1) << (30 - t))
        cand_f = _key_to_float(cand)
        return jnp.where(count(sc_ref, lambda blk, k0: blk >= cand_f) >= kf, cand, key)

    key = lax.fori_loop(0, 15, bit_step_upper, key)
    key = lax.fori_loop(15, 31, bit_step, key)
    has_thr = key != INT_MIN
    thr = jnp.where(has_thr, _key_to_float(key), -jnp.inf)
    thr_ref[...] = thr
    need = kf - count(sc_ref, lambda blk, k0: blk > thr)
    c_eq = count(sc_ref, lambda blk, k0: blk == thr)
    cut_ref[...] = jnp.where(has_thr, jnp.int32(2 ** 30), jnp.int32(-1))
    split = has_thr & (c_eq > need)

    @pl.when(jnp.max(jnp.where(split, 1.0, 0.0)) > 0.0)
    def _():
        sub_iota = lax.broadcasted_iota(jnp.int32, (SUBLANES, 1), 0)

        def idx_step(t, cut):
            cand = cut | (jnp.int32(1) << (13 - t))
            below = count(sc_ref, lambda blk, k0: (blk == thr) & (k0 + sub_iota < cand))
            return jnp.where(below <= need - 1.0, cand, cut)

        cut = lax.fori_loop(0, 14, idx_step, jnp.zeros((SUBLANES, tq), jnp.int32))
        cut_ref[...] = jnp.where(has_thr, cut, -1)

    def masked_bias(ks):
        scb = sc_ref[pl.ds(ks, tk), :]
        thr_b = thr_ref[:1, :]
        sel = (scb > thr_b) | ((scb == thr_b) & (ks + key_iota <= cut_ref[:1, :]))
        bias = jnp.where(sel, 0.0, NEG_BIG).T
        return jnp.concatenate([bias, bias], axis=0)

    def logits(p, kab):
        return lax.dot_general(qam_ref[p], kab[:, p * PAIR:(p + 1) * PAIR], _NT,
                               preferred_element_type=F32)

    bound = bound_ref[0]
    shift_ref[...] = jnp.full(shift_ref.shape, bound, F32)

    @pl.when(bound > SOFTMAX_SAFE_BOUND)
    def _():
        l_ref[...] = jnp.full(l_ref.shape, NEG_BIG, F32)

        def max_block(j, carry):
            ks = pl.multiple_of(j * tk, tk)
            bias = masked_bias(ks)
            kab = ka_ref[pl.ds(ks, tk), :]
            for p in range(N_PAIRS):
                s = logits(p, kab) + bias
                mx = l_ref[p]
                for u in range(reps):
                    mx = jnp.maximum(mx, s[:, u * LANES:(u + 1) * LANES])
                l_ref[p] = mx
            return carry

        lax.fori_loop(0, nkb, max_block, 0)
        for p in range(N_PAIRS):
            shift_ref[p] = jnp.broadcast_to(jnp.max(l_ref[p], axis=1, keepdims=True),
                                            (2 * tq, LANES))

    l_ref[...] = jnp.zeros(l_ref.shape, F32)
    acc_ref[...] = jnp.zeros(acc_ref.shape, F32)

    def attend_block(j, carry):
        ks = pl.multiple_of(j * tk, tk)
        bias = masked_bias(ks)
        kab = ka_ref[pl.ds(ks, tk), :]
        vab = va_ref[pl.ds(ks, tk), :]
        for p in range(N_PAIRS):
            e = jnp.exp((logits(p, kab) - _lanes(shift_ref[p], reps)) + bias)
            part = l_ref[p]
            for u in range(reps):
                part = part + e[:, u * LANES:(u + 1) * LANES]
            l_ref[p] = part
            acc_ref[p] = acc_ref[p] + jnp.dot(
                e.astype(BF16), vab[:, p * PAIR:(p + 1) * PAIR], preferred_element_type=F32)
        return carry

    lax.fori_loop(0, nkb, attend_block, 0)

    lane_o = lax.broadcasted_iota(jnp.int32, (tq, PAIR), 1)
    for p in range(N_PAIRS):
        out = acc_ref[p] / jnp.sum(l_ref[p], axis=1, keepdims=True)
        o_ref[:, p * PAIR:(p + 1) * PAIR] = jnp.where(
            lane_o < HEAD_DIM, out[:tq], out[tq:]).astype(BF16)


def _dsa(bound, qa16, qi16, wi, ki16, ka16, va16, *, tq, tk, n_valid, q_off):
    b, lq, _ = qa16.shape
    lk = ka16.shape[1]
    topk = min(TOPK_MAX, n_valid // 4)
    lq_pad = pl.cdiv(lq, LANES) * LANES
    if lq_pad != lq:
        pad = lambda a: jnp.pad(a, ((0, 0), (0, lq_pad - lq), (0, 0)))
        qa16, qi16, wi = pad(qa16), pad(qi16), pad(wi)
    tq = min(tq, lq_pad)
    assert tq % LANES == 0 and lq_pad % tq == 0 and lk % tk == 0
    qblk = lambda bb, i: (bb, i, 0)
    kblk = lambda bb, i: (bb, 0, 0)
    kern = functools.partial(_dsa_kernel, tq=tq, tk=tk, n_valid=n_valid, q_off=q_off, topk=topk)
    stacked = lambda w, dt: pltpu.VMEM((N_PAIRS, 2 * tq, w), dt)
    out = pl.pallas_call(
        kern,
        grid=(b, lq_pad // tq),
        in_specs=[pl.BlockSpec(memory_space=pltpu.SMEM),
                  pl.BlockSpec((None, tq, WIDTH), qblk),
                  pl.BlockSpec((None, tq, WIDTH), qblk),
                  pl.BlockSpec((None, tq, LANES), qblk),
                  _per_batch((None, lk, LANES), kblk, lq_pad // tq),
                  _per_batch((None, lk, WIDTH), kblk, lq_pad // tq),
                  _per_batch((None, lk, WIDTH), kblk, lq_pad // tq)],
        out_specs=pl.BlockSpec((None, tq, WIDTH), qblk),
        out_shape=jax.ShapeDtypeStruct((b, lq_pad, WIDTH), BF16),
        scratch_shapes=[pltpu.VMEM((lk, tq), F32),
                        pltpu.VMEM((lk, tq), BF16),
                        stacked(PAIR, BF16),
                        stacked(PAIR, BF16),
                        pltpu.VMEM((N_PAIRS, SUBLANES, 2 * tq), F32),
                        stacked(LANES, F32),
                        stacked(LANES, F32),
                        stacked(PAIR, F32),
                        pltpu.VMEM((SUBLANES, tq), F32),
                        pltpu.VMEM((SUBLANES, tq), jnp.int32)],
        compiler_params=pltpu.CompilerParams(dimension_semantics=("arbitrary", "arbitrary"),
                                             vmem_limit_bytes=VMEM_LIMIT),
        name="dsa",
    )(bound, qa16, qi16, wi, ki16, ka16, va16)
    return out[:, :lq]


def _sb_kernel(qb_ref, kb_ref, vb_ref, tri_ref, o_ref, qbm_ref, tail_ref, acc_ref,
               *, tq, tk, q_off):
    i = pl.program_id(1)
    q0 = q_off + i * tq
    qpos = q0 + lax.broadcasted_iota(jnp.int32, (tq, 1), 0)
    qpos = jnp.concatenate([qpos, qpos], axis=0)
    lane_kpos = lax.broadcasted_iota(jnp.int32, (1, tk), 1)
    tri = tri_ref[...]

    _mask_heads(qb_ref, qbm_ref)
    tail_ref[...] = jnp.zeros(tail_ref.shape, F32)
    acc_ref[...] = jnp.zeros(acc_ref.shape, F32)

    def block(state):
        j, _ = state
        ks = pl.multiple_of(j * tk, tk)
        causal = (ks + lane_kpos) < qpos
        kbb = kb_ref[pl.ds(ks, tk), :]
        vbb = vb_ref[pl.ds(ks, tk), :]
        tail_min = jnp.full((2 * tq, 1), jnp.inf, F32)
        for p in range(N_PAIRS):
            cols = slice(p * PAIR, (p + 1) * PAIR)
            z = lax.dot_general(qbm_ref[p], kbb[:, cols], _NT, preferred_element_type=F32)
            lp = jnp.log(1.0 + jnp.exp(-jnp.abs(z)))
            sp = jnp.where(causal, jnp.maximum(z, 0.0) + lp, 0.0)
            hi = sp.astype(BF16)
            lo = (sp - hi.astype(F32)).astype(BF16)
            inner = (jnp.dot(hi, tri, preferred_element_type=F32)
                     + jnp.dot(lo, tri, preferred_element_type=F32))
            tail_old = tail_ref[p]
            log_a = (jnp.minimum(z, 0.0) - lp) - inner - tail_old[:, :1]
            a = jnp.where(causal, jnp.exp(log_a), 0.0)
            acc_ref[p] = acc_ref[p] + jnp.dot(a.astype(BF16), vbb[:, cols],
                                              preferred_element_type=F32)
            tail_new = tail_old + jnp.sum(sp, axis=1, keepdims=True)
            tail_ref[p] = tail_new
            tail_min = jnp.minimum(tail_min, tail_new[:, :1])
        return j - 1, jnp.min(tail_min)

    def more(state):
        j, tail_min = state
        return (j >= 0) & (tail_min <= SB_EXIT_TAIL)

    lax.while_loop(more, block, ((q0 + tq - 1) // tk, jnp.float32(0.0)))

    lane = lax.broadcasted_iota(jnp.int32, (tq, PAIR), 1)
    for p in range(N_PAIRS):
        acc = acc_ref[p]
        o_ref[:, p * PAIR:(p + 1) * PAIR] = jnp.where(
            lane < HEAD_DIM, acc[:tq], acc[tq:]).astype(BF16)


def _sb(qb16, kb16, vb16, tri, *, tq, tk, q_off):
    b, lq, _ = qb16.shape
    lk = kb16.shape[1]
    qblk = lambda bb, i: (bb, i, 0)
    kblk = lambda bb, i: (bb, 0, 0)
    kern = functools.partial(_sb_kernel, tq=tq, tk=tk, q_off=q_off)
    return pl.pallas_call(
        kern,
        grid=(b, lq // tq),
        in_specs=[pl.BlockSpec((None, tq, WIDTH), qblk),
                  _per_batch((None, lk, WIDTH), kblk, lq // tq),
                  _per_batch((None, lk, WIDTH), kblk, lq // tq),
                  _resident((tk, tk), lambda bb, i: (0, 0))],
        out_specs=pl.BlockSpec((None, tq, WIDTH), qblk),
        out_shape=jax.ShapeDtypeStruct((b, lq, WIDTH), BF16),
        scratch_shapes=[pltpu.VMEM((N_PAIRS, 2 * tq, PAIR), BF16),
                        pltpu.VMEM((N_PAIRS, 2 * tq, LANES), F32),
                        pltpu.VMEM((N_PAIRS, 2 * tq, PAIR), F32)],
        compiler_params=pltpu.CompilerParams(dimension_semantics=("arbitrary", "arbitrary"),
                                             vmem_limit_bytes=VMEM_LIMIT),
        name="sb",
    )(qb16, kb16, vb16, tri)


FF_CHUNK = 1024


def _rms_rows(x, g):
    return x * lax.rsqrt(jnp.mean(x * x, axis=-1, keepdims=True) + EPS) * g


def _merge_kernel(x_ref, oa_ref, ob_ref, gmix_ref, gffn_ref, wg_ref, wba_ref, wbb_ref,
                  wout_ref, wup_ref, wdown_ref, y_ref):
    x = x_ref[...]
    hx = _rms_rows(x, gmix_ref[...]).astype(BF16)
    gate_a = jnp.dot(hx, wg_ref[:, :D_MODEL], preferred_element_type=F32)
    gate_b = jnp.dot(hx, wg_ref[:, D_MODEL:], preferred_element_type=F32)
    pa = jnp.dot(oa_ref[...], wba_ref[...], preferred_element_type=F32)
    pb = jnp.dot(ob_ref[...], wbb_ref[...], preferred_element_type=F32)
    m = jax.nn.sigmoid(gate_a) * pa + jax.nn.sigmoid(gate_b) * pb
    h = x + jnp.dot(m.astype(BF16), wout_ref[...], preferred_element_type=F32)
    hn = _rms_rows(h, gffn_ref[...]).astype(BF16)
    y = h
    for c in range(D_FF // FF_CHUNK):
        u = jnp.dot(hn, wup_ref[:, c * FF_CHUNK:(c + 1) * FF_CHUNK], preferred_element_type=F32)
        r = jnp.square(jnp.maximum(u, 0.0)).astype(BF16)
        y = y + jnp.dot(r, wdown_ref[c * FF_CHUNK:(c + 1) * FF_CHUNK, :],
                        preferred_element_type=F32)
    y_ref[...] = y


def _merge_ffn(x2d, oa16, ob16, wts, tm):
    n = x2d.shape[0]
    row = lambda i: (i, 0)
    const = lambda i: (0, 0)
    weights = [wts["w_gate"], wts["w_branch_a"], wts["w_branch_b"], wts["w_out"],
               wts["w_up"], wts["w_down"]]
    return pl.pallas_call(
        _merge_kernel,
        grid=(n // tm,),
        in_specs=[pl.BlockSpec((tm, D_MODEL), row),
                  pl.BlockSpec((tm, WIDTH), row),
                  pl.BlockSpec((tm, WIDTH), row),
                  pl.BlockSpec((1, D_MODEL), const),
                  pl.BlockSpec((1, D_MODEL), const)]
                 + [_resident(w.shape, const) for w in weights],
        out_specs=pl.BlockSpec((tm, D_MODEL), row),
        out_shape=jax.ShapeDtypeStruct((n, D_MODEL), F32),
        compiler_params=pltpu.CompilerParams(dimension_semantics=("arbitrary",),
                                             vmem_limit_bytes=VMEM_LIMIT),
        name="merge_ffn",
    )(x2d, oa16, ob16, wts["g_mix"], wts["g_ffn"], *weights)


def _rope_tables(pos, rows):
    half = HEAD_DIM // 2
    inv_freq = jnp.power(ROPE_THETA, -jnp.arange(half, dtype=F32) / half)
    ang = pos.astype(F32)[:, None] * inv_freq[None, :]
    cos, sin = jnp.cos(ang), jnp.sin(ang)
    cs = jnp.concatenate([cos, cos, cos, cos], axis=1)
    sn = jnp.concatenate([-sin, sin, -sin, sin], axis=1)
    reps = rows // pos.shape[0]
    if reps > 1:
        cs, sn = jnp.tile(cs, (reps, 1)), jnp.tile(sn, (reps, 1))
    return cs, sn


def _prep_weights(g_mix, w_in, g_qn, g_kn, w_branch_a, w_branch_b, w_out, g_ffn, w_up, w_down):
    offs = [int(v) for v in np.cumsum(SPLIT_SIZES)[:-1]]
    w_qa, w_ka, w_va, w_qi, w_ki, w_wi, w_qb, w_kb, w_vb, w_ga, w_gb = jnp.split(w_in, offs, axis=1)
    pad = jnp.zeros((D_MODEL, LANES - IDX_DIM - N_HEADS), w_in.dtype)
    w_proj = jnp.concatenate([w_qa, w_ka, w_va, w_qi, w_qb, w_kb, w_vb, w_ki, w_wi, pad], axis=1)
    head = np.arange(WIDTH) // HEAD_DIM
    return {
        "w_proj": w_proj.astype(BF16),
        "w_gate": jnp.concatenate([w_ga, w_gb], axis=1).astype(BF16),
        "grp": jnp.asarray(head[:, None] == head[None, :], BF16),
        "g_mix": g_mix.reshape(1, D_MODEL),
        "g_ffn": g_ffn.reshape(1, D_MODEL),
        "g_qn": jnp.tile(g_qn, N_HEADS).reshape(1, WIDTH),
        "g_kn": jnp.tile(g_kn, N_HEADS).reshape(1, WIDTH),
        "logit_bound": (HEAD_DIM * QK_SCALE * jnp.max(jnp.abs(g_qn))
                        * jnp.max(jnp.abs(g_kn))).reshape(1).astype(F32),
        "w_branch_a": w_branch_a.astype(BF16),
        "w_branch_b": w_branch_b.astype(BF16),
        "w_out": w_out.astype(BF16),
        "w_up": w_up.astype(BF16),
        "w_down": w_down.astype(BF16),
    }


def _tri(tk):
    r = np.arange(tk)
    return jnp.asarray(r[:, None] > r[None, :], BF16)


def _pad_keys(a, lk):
    return jnp.pad(a, ((0, 0), (0, lk - a.shape[1]), (0, 0)))


def _layer(x, pos, wts, caches, *, tq_dsa, tk_dsa, tq_sb, tk_sb):
    b, l, _ = x.shape
    n = b * l
    tm = min(512, n)
    x2d = x.reshape(n, D_MODEL)
    rows = l if l >= tm else tm
    (ka, va, ki, kb, vb, qa16, ka16, va16, qi16, ki16, qb16, kb16, vb16, wi) = _project(
        x2d, _rope_tables(pos, rows), wts, tm)
    per_seq = lambda a: a.reshape(b, l, a.shape[-1])
    qa16, qi16, qb16, wi = per_seq(qa16), per_seq(qi16), per_seq(qb16), per_seq(wi)
    ka16, va16, ki16, kb16, vb16 = (per_seq(a) for a in (ka16, va16, ki16, kb16, vb16))
    past = 0
    if caches is not None:
        c_ka, c_va, c_ki, c_kb, c_vb = caches
        past = c_ka.shape[1]
        flat = lambda c: c.reshape(b, past, -1).astype(BF16)
        c_ki16 = c_ki.astype(BF16)
        ka16 = jnp.concatenate([flat(c_ka), ka16], axis=1)
        va16 = jnp.concatenate([flat(c_va), va16], axis=1)
        ki16 = jnp.concatenate([jnp.concatenate([c_ki16, c_ki16], axis=2), ki16], axis=1)
        kb16 = jnp.concatenate([flat(c_kb), kb16], axis=1)
        vb16 = jnp.concatenate([flat(c_vb), vb16], axis=1)
    n_valid = past + l
    lk_dsa = pl.cdiv(n_valid, tk_dsa) * tk_dsa
    lk_sb = pl.cdiv(n_valid, tk_sb) * tk_sb
    oa16 = _dsa(wts["logit_bound"], qa16, qi16, wi, _pad_keys(ki16, lk_dsa), _pad_keys(ka16, lk_dsa),
                _pad_keys(va16, lk_dsa), tq=tq_dsa, tk=tk_dsa, n_valid=n_valid, q_off=past)
    ob16 = _sb(qb16, _pad_keys(kb16, lk_sb), _pad_keys(vb16, lk_sb), _tri(tk_sb),
               tq=tq_sb, tk=tk_sb, q_off=past)
    y = _merge_ffn(x2d, oa16.reshape(n, WIDTH), ob16.reshape(n, WIDTH), wts, tm)
    heads = lambda a: a.reshape(b, l, N_HEADS, HEAD_DIM)
    return (y.reshape(b, l, D_MODEL), heads(ka), heads(va), ki.reshape(b, l, IDX_DIM),
            heads(kb), heads(vb))


def kernel(x_prompt, x_sample, cache_k_a, cache_v_a, cache_k_idx, cache_k_sb, cache_v_sb,
           g_mix, w_in, g_qn, g_kn, w_branch_a, w_branch_b, w_out, g_ffn, w_up, w_down):
    wts = _prep_weights(g_mix, w_in, g_qn, g_kn, w_branch_a, w_branch_b, w_out, g_ffn, w_up, w_down)
    s = x_prompt.shape[1]
    t = x_sample.shape[1]
    past = cache_k_a.shape[1]
    prompt = _layer(x_prompt, jnp.arange(s, dtype=jnp.int32), wts, None,
                    tq_dsa=min(256, s), tk_dsa=min(512, s), tq_sb=min(256, s), tk_sb=min(256, s))
    sample = _layer(x_sample, past + jnp.arange(t, dtype=jnp.int32), wts,
                    (cache_k_a, cache_v_a, cache_k_idx, cache_k_sb, cache_v_sb),
                    tq_dsa=LANES, tk_dsa=512, tq_sb=t, tk_sb=128)
    return (prompt[0], sample[0]) + prompt[1:] + sample[1:]
```

```python
import functools

import numpy as np
import jax
import jax.numpy as jnp
from jax import lax
from jax.experimental import pallas as pl
from jax.experimental.pallas import tpu as pltpu

D_MODEL = 1024
HEAD_DIM = 64
N_HEADS = 8
WIDTH = N_HEADS * HEAD_DIM
IDX_DIM = 64
CHUNK = 64
CHUNK_SHIFT = CHUNK.bit_length() - 1
assert 1 << CHUNK_SHIFT == CHUNK
TOPK_MAX = 256
D_FF = 4 * D_MODEL
ROPE_THETA = 10000.0
EPS = 1e-6
SPLIT_SIZES = (WIDTH, WIDTH, WIDTH, N_HEADS * IDX_DIM, IDX_DIM, N_HEADS,
               WIDTH, WIDTH, WIDTH, D_MODEL, D_MODEL)

LANES = 128
PAIR = 2 * HEAD_DIM
N_PAIRS = N_HEADS // 2
QK_SCALE = HEAD_DIM ** -0.5
W_IDX_SCALE = (N_HEADS ** -0.5) * (IDX_DIM ** -0.5)
INT_MIN = -2 ** 31
NEG_BIG = -1e30
SOFTMAX_SAFE_BOUND = 40.0
SB_EXIT_TAIL = 105.0
VMEM_LIMIT = 56 * 1024 * 1024

F32 = jnp.float32
BF16 = jnp.bfloat16
_NT = (((1,), (1,)), ((), ()))


def _resident(shape, index_map):
    return pl.BlockSpec(shape, index_map, pipeline_mode=pl.Buffered(1))


def _per_batch(shape, index_map, steps_per_batch):
    if steps_per_batch == 1:
        return pl.BlockSpec(shape, index_map)
    return _resident(shape, index_map)


def _rope(x, cs, sn):
    w = x.shape[1]
    lane = lax.broadcasted_iota(jnp.int32, x.shape, 1)
    first_half = (lane & (HEAD_DIM // 2)) == 0
    partner = jnp.where(first_half, pltpu.roll(x, w - HEAD_DIM // 2, 1),
                        pltpu.roll(x, HEAD_DIM // 2, 1))
    reps = w // LANES
    if reps > 1:
        cs = jnp.concatenate([cs] * reps, axis=1)
        sn = jnp.concatenate([sn] * reps, axis=1)
    return x * cs + partner * sn


def _head_rmsnorm(z, grp, g):
    zz = z * z
    hi = zz.astype(BF16)
    lo = (zz - hi.astype(F32)).astype(BF16)
    ss = (jnp.dot(hi, grp, preferred_element_type=F32)
          + jnp.dot(lo, grp, preferred_element_type=F32))
    return z * lax.rsqrt(ss * (1.0 / HEAD_DIM) + EPS) * g


def _proj_kernel(x_ref, gmix_ref, w_ref, gqn_ref, gkn_ref, cs_ref, sn_ref, grp_ref,
                 ka_ref, va_ref, ki_ref, kb_ref, vb_ref,
                 qa16_ref, ka16_ref, va16_ref, qi16_ref, ki16_ref,
                 qb16_ref, kb16_ref, vb16_ref, wi_ref):
    x = x_ref[...]
    h = (x * lax.rsqrt(jnp.mean(x * x, axis=-1, keepdims=True) + EPS)
         * gmix_ref[...]).astype(BF16)
    cs = cs_ref[...]
    sn = sn_ref[...]
    grp = grp_ref[...]

    def seg(k, width=WIDTH):
        return jnp.dot(h, w_ref[:, k * WIDTH:k * WIDTH + width],
                       preferred_element_type=F32)

    split_heads = lambda z: pltpu.einshape("m(hd)->mhd", z, h=N_HEADS)

    qa = _rope(_head_rmsnorm(seg(0), grp, gqn_ref[...]), cs, sn)
    qa16_ref[...] = (qa * QK_SCALE).astype(BF16)
    ka = _rope(_head_rmsnorm(seg(1), grp, gkn_ref[...]), cs, sn)
    ka_ref[...] = split_heads(ka)
    ka16_ref[...] = ka.astype(BF16)
    va = seg(2)
    va_ref[...] = split_heads(va)
    va16_ref[...] = va.astype(BF16)
    qi16_ref[...] = _rope(seg(3), cs, sn).astype(BF16)
    qb16_ref[...] = (seg(4) * QK_SCALE).astype(BF16)
    kb = seg(5)
    kb_ref[...] = split_heads(kb)
    kb16_ref[...] = kb.astype(BF16)
    vb = seg(6)
    vb_ref[...] = split_heads(vb)
    vb16_ref[...] = vb.astype(BF16)
    misc = seg(7, LANES)
    ki = _rope(misc, cs, sn)[:, :IDX_DIM]
    ki_ref[...] = ki
    ki16 = ki.astype(BF16)
    ki16_ref[...] = jnp.concatenate([ki16, ki16], axis=1)
    wi_ref[...] = misc * W_IDX_SCALE


def _project(x2d, tabs, wts, tm):
    n = x2d.shape[0]
    cs, sn = tabs
    nt = cs.shape[0] // tm
    row = lambda i: (i, 0)
    const = lambda i: (0, 0)
    tab = lambda i: (i % nt, 0)
    f32_leaf = lambda w: jax.ShapeDtypeStruct((n, w), F32)
    heads_leaf = jax.ShapeDtypeStruct((n, N_HEADS, HEAD_DIM), F32)
    b16 = lambda w: jax.ShapeDtypeStruct((n, w), BF16)
    out_shape = [heads_leaf, heads_leaf, f32_leaf(IDX_DIM), heads_leaf, heads_leaf,
                 b16(WIDTH), b16(WIDTH), b16(WIDTH), b16(WIDTH), b16(LANES),
                 b16(WIDTH), b16(WIDTH), b16(WIDTH), f32_leaf(LANES)]
    out_specs = [pl.BlockSpec((tm,) + s.shape[1:], lambda i, nd=len(s.shape): (i,) + (0,) * (nd - 1))
                 for s in out_shape]
    w_proj = wts["w_proj"]
    return pl.pallas_call(
        _proj_kernel,
        grid=(n // tm,),
        in_specs=[pl.BlockSpec((tm, D_MODEL), row),
                  pl.BlockSpec((1, D_MODEL), const),
                  _resident(w_proj.shape, const),
                  pl.BlockSpec((1, WIDTH), const),
                  pl.BlockSpec((1, WIDTH), const),
                  pl.BlockSpec((tm, LANES), tab),
                  pl.BlockSpec((tm, LANES), tab),
                  _resident((WIDTH, WIDTH), const)],
        out_specs=out_specs,
        out_shape=out_shape,
        compiler_params=pltpu.CompilerParams(dimension_semantics=("arbitrary",),
                                             vmem_limit_bytes=VMEM_LIMIT),
        name="proj",
    )(x2d, wts["g_mix"], w_proj, wts["g_qn"], wts["g_kn"], cs, sn, wts["grp"])


SUBLANES = 8
DSA_GROUP = 4
COUNT_CHAINS = 4


HALF_MIN = -2 ** 15


def _sort_key(x):
    b = lax.bitcast_convert_type(x, jnp.int32)
    return b ^ ((b >> 31) & 0x7FFFFFFF)


def _key_to_float(k):
    return lax.bitcast_convert_type(k ^ ((k >> 31) & 0x7FFFFFFF), F32)


def _mask_heads(src_ref, dst_ref, group=2):
    tq = src_ref.shape[0]
    gw = group * HEAD_DIM
    head = lax.broadcasted_iota(jnp.int32, (tq, gw), 1) // HEAD_DIM
    for g in range(N_HEADS // group):
        cols = src_ref[:, g * gw:(g + 1) * gw]
        zero = jnp.zeros_like(cols)
        for h in range(group):
            dst_ref[g, h * tq:(h + 1) * tq, :] = jnp.where(head == h, cols, zero)


def _lanes(x, reps):
    return jnp.concatenate([x] * reps, axis=1) if reps > 1 else x


def _dsa_kernel(bound_ref, qa_ref, qi_ref, wi_ref, ki_ref, ka_ref, va_ref, o_ref,
                sc_ref, hi_ref, lo_ref, qam_ref, qim_ref, w_ref, shift_ref, l_ref, acc_ref,
                thr_ref, cut_ref, *, tq, tk, n_valid, q_off, topk, group):
    i = pl.program_id(1)
    q0 = q_off + i * tq
    qpos = q0 + lax.broadcasted_iota(jnp.int32, (1, tq), 1)
    chunk_end = lambda pos: ((pos >> CHUNK_SHIFT) + 1) << CHUNK_SHIFT
    lim = jnp.minimum(chunk_end(qpos), n_valid)
    n_adm = jnp.minimum(chunk_end(q0 + tq - 1), n_valid)
    nkb = (n_adm + tk - 1) // tk
    reps = tk // LANES
    gw = group * HEAD_DIM
    n_groups = N_HEADS // group
    key_iota = lax.broadcasted_iota(jnp.int32, (tk, 1), 0)

    _mask_heads(qi_ref, qim_ref, group)
    _mask_heads(qa_ref, qam_ref, group)
    w_t = wi_ref[...].T
    for g in range(n_groups):
        r0 = IDX_DIM + group * g
        row = jnp.concatenate([w_t[r0 + h:r0 + h + 1] for h in range(group)], axis=1)
        w_ref[g] = jnp.broadcast_to(row, (SUBLANES, group * tq))

    def score_block(j, carry):
        ks = pl.multiple_of(j * tk, tk)
        kib = _lanes(ki_ref[pl.ds(ks, tk), :], gw // LANES)
        score = jnp.zeros((tk, tq), F32)
        for g in range(n_groups):
            s = lax.dot_general(kib, qim_ref[g], _NT, preferred_element_type=F32)
            r = jnp.maximum(s, 0.0) * w_ref[g][:1]
            for h in range(group):
                score = score + r[:, h * tq:(h + 1) * tq]
        admissible = ks + key_iota < lim
        sc_ref[pl.ds(ks, tk), :] = jnp.where(admissible, score, -jnp.inf)
        key = jnp.where(admissible, _sort_key(score), INT_MIN)
        hi_ref[pl.ds(ks, tk), :] = (key >> 16).astype(jnp.int16)
        lo_ref[pl.ds(ks, tk), :] = ((key & 0xFFFF) + HALF_MIN).astype(jnp.int16)
        return carry

    lax.fori_loop(0, nkb, score_block, 0)

    def count(src_ref, pred):
        dt = src_ref.dtype
        sub = SUBLANES * 4 // jnp.dtype(dt).itemsize
        one, zero = jnp.ones((), dt), jnp.zeros((), dt)

        def body(c, acc):
            ks = pl.multiple_of(c * tk, tk)
            blk = src_ref[pl.ds(ks, tk), :]
            parts = [jnp.zeros((sub, tq), dt) for _ in range(COUNT_CHAINS)]
            for r in range(tk // sub):
                hit = pred(blk[r * sub:(r + 1) * sub], ks + r * sub)
                parts[r % COUNT_CHAINS] = parts[r % COUNT_CHAINS] + jnp.where(hit, one, zero)
            part = functools.reduce(lambda a, b: a + b, parts).astype(F32)
            for half in range(sub // SUBLANES - 1):
                part = part[:SUBLANES] + part[SUBLANES:]
            return acc + part

        acc = lax.fori_loop(0, nkb, body, jnp.zeros((SUBLANES, tq), F32))
        return jnp.broadcast_to(jnp.sum(acc, axis=0, keepdims=True), (SUBLANES, tq))

    def halves(v):
        u = v.astype(jnp.int16)
        return jnp.concatenate([u, u], axis=0)

    def largest_half(src_ref, wanted):
        nonneg = count(src_ref, lambda blk, k0: blk >= jnp.int16(0)) >= wanted
        h0 = jnp.where(nonneg, jnp.zeros((SUBLANES, tq), jnp.int32),
                       jnp.full((SUBLANES, tq), HALF_MIN, jnp.int32))

        def bit_step(t, h):
            cand = h | (jnp.int32(1) << (14 - t))
            cand16 = halves(cand)
            return jnp.where(count(src_ref, lambda blk, k0: blk >= cand16) >= wanted, cand, h)

        return lax.fori_loop(0, 15, bit_step, h0)

    kf = float(topk)
    hi = largest_half(hi_ref, kf)
    hi16 = halves(hi)
    wanted_lo = kf - count(hi_ref, lambda blk, k0: blk > hi16)

    def keep_bucket(c, carry):
        ks = pl.multiple_of(c * tk, tk)
        in_bucket = hi_ref[pl.ds(ks, tk), :] == hi16[:1]
        lo_ref[pl.ds(ks, tk), :] = jnp.where(in_bucket, lo_ref[pl.ds(ks, tk), :],
                                             jnp.int16(HALF_MIN))
        return carry

    lax.fori_loop(0, nkb, keep_bucket, 0)
    lo = largest_half(lo_ref, wanted_lo)
    key = (hi << 16) | ((lo - HALF_MIN) & 0xFFFF)
    has_thr = key != INT_MIN
    thr = jnp.where(has_thr, _key_to_float(key), -jnp.inf)
    thr_ref[...] = thr
    need = kf - count(sc_ref, lambda blk, k0: blk > thr)
    c_eq = count(sc_ref, lambda blk, k0: blk == thr)
    cut_ref[...] = jnp.where(has_thr, jnp.int32(2 ** 30), jnp.int32(-1))
    split = has_thr & (c_eq > need)

    @pl.when(jnp.max(jnp.where(split, 1.0, 0.0)) > 0.0)
    def _():
        sub_iota = lax.broadcasted_iota(jnp.int32, (SUBLANES, 1), 0)

        def idx_step(t, cut):
            cand = cut | (jnp.int32(1) << (13 - t))
            below = count(sc_ref, lambda blk, k0: (blk == thr) & (k0 + sub_iota < cand))
            return jnp.where(below <= need - 1.0, cand, cut)

        cut = lax.fori_loop(0, 14, idx_step, jnp.zeros((SUBLANES, tq), jnp.int32))
        cut_ref[...] = jnp.where(has_thr, cut, -1)

    def masked_bias(ks):
        scb = sc_ref[pl.ds(ks, tk), :]
        thr_b = thr_ref[:1, :]
        sel = (scb > thr_b) | ((scb == thr_b) & (ks + key_iota <= cut_ref[:1, :]))
        bias = jnp.where(sel, 0.0, NEG_BIG).T
        return jnp.concatenate([bias] * group, axis=0)

    def logits(g, kab):
        return lax.dot_general(qam_ref[g], kab[:, g * gw:(g + 1) * gw], _NT,
                               preferred_element_type=F32)

    bound = bound_ref[0]
    shift_ref[...] = jnp.full(shift_ref.shape, bound, F32)

    @pl.when(bound > SOFTMAX_SAFE_BOUND)
    def _():
        l_ref[...] = jnp.full(l_ref.shape, NEG_BIG, F32)

        def max_block(j, carry):
            ks = pl.multiple_of(j * tk, tk)
            bias = masked_bias(ks)
            kab = ka_ref[pl.ds(ks, tk), :]
            for g in range(n_groups):
                s = logits(g, kab) + bias
                mx = l_ref[g]
                for u in range(reps):
                    mx = jnp.maximum(mx, s[:, u * LANES:(u + 1) * LANES])
                l_ref[g] = mx
            return carry

        lax.fori_loop(0, nkb, max_block, 0)
        for g in range(n_groups):
            shift_ref[g] = jnp.broadcast_to(jnp.max(l_ref[g], axis=1, keepdims=True),
                                            (group * tq, LANES))

    l_ref[...] = jnp.zeros(l_ref.shape, F32)
    acc_ref[...] = jnp.zeros(acc_ref.shape, F32)

    def attend_block(j, carry):
        ks = pl.multiple_of(j * tk, tk)
        bias = masked_bias(ks)
        kab = ka_ref[pl.ds(ks, tk), :]
        vab = va_ref[pl.ds(ks, tk), :]
        for g in range(n_groups):
            e = jnp.exp((logits(g, kab) - _lanes(shift_ref[g], reps)) + bias)
            part = l_ref[g]
            for u in range(reps):
                part = part + e[:, u * LANES:(u + 1) * LANES]
            l_ref[g] = part
            acc_ref[g] = acc_ref[g] + jnp.dot(
                e.astype(BF16), vab[:, g * gw:(g + 1) * gw], preferred_element_type=F32)
        return carry

    lax.fori_loop(0, nkb, attend_block, 0)

    head_o = lax.broadcasted_iota(jnp.int32, (tq, gw), 1) // HEAD_DIM
    for g in range(n_groups):
        out = acc_ref[g] / jnp.sum(l_ref[g], axis=1, keepdims=True)
        res = out[:tq]
        for h in range(1, group):
            res = jnp.where(head_o == h, out[h * tq:(h + 1) * tq], res)
        o_ref[:, g * gw:(g + 1) * gw] = res.astype(BF16)


def _dsa(bound, qa16, qi16, wi, ki16, ka16, va16, *, tq, tk, n_valid, q_off, group=DSA_GROUP):
    b, lq, _ = qa16.shape
    lk = ka16.shape[1]
    topk = min(TOPK_MAX, n_valid // 4)
    lq_pad = pl.cdiv(lq, LANES) * LANES
    if lq_pad != lq:
        pad = lambda a: jnp.pad(a, ((0, 0), (0, lq_pad - lq), (0, 0)))
        qa16, qi16, wi = pad(qa16), pad(qi16), pad(wi)
    tq = min(tq, lq_pad)
    assert tq % LANES == 0 and lq_pad % tq == 0 and lk % tk == 0
    qblk = lambda bb, i: (bb, i, 0)
    kblk = lambda bb, i: (bb, 0, 0)
    kern = functools.partial(_dsa_kernel, tq=tq, tk=tk, n_valid=n_valid, q_off=q_off, topk=topk,
                             group=group)
    n_groups, gw = N_HEADS // group, group * HEAD_DIM
    stacked = lambda w, dt: pltpu.VMEM((n_groups, group * tq, w), dt)
    out = pl.pallas_call(
        kern,
        grid=(b, lq_pad // tq),
        in_specs=[pl.BlockSpec(memory_space=pltpu.SMEM),
                  pl.BlockSpec((None, tq, WIDTH), qblk),
                  pl.BlockSpec((None, tq, WIDTH), qblk),
                  pl.BlockSpec((None, tq, LANES), qblk),
                  _per_batch((None, lk, LANES), kblk, lq_pad // tq),
                  _per_batch((None, lk, WIDTH), kblk, lq_pad // tq),
                  _per_batch((None, lk, WIDTH), kblk, lq_pad // tq)],
        out_specs=pl.BlockSpec((None, tq, WIDTH), qblk),
        out_shape=jax.ShapeDtypeStruct((b, lq_pad, WIDTH), BF16),
        scratch_shapes=[pltpu.VMEM((lk, tq), F32),
                        pltpu.VMEM((lk, tq), jnp.int16),
                        pltpu.VMEM((lk, tq), jnp.int16),
                        stacked(gw, BF16),
                        stacked(gw, BF16),
                        pltpu.VMEM((n_groups, SUBLANES, group * tq), F32),
                        stacked(LANES, F32),
                        stacked(LANES, F32),
                        stacked(gw, F32),
                        pltpu.VMEM((SUBLANES, tq), F32),
                        pltpu.VMEM((SUBLANES, tq), jnp.int32)],
        compiler_params=pltpu.CompilerParams(dimension_semantics=("arbitrary", "arbitrary"),
                                             vmem_limit_bytes=VMEM_LIMIT),
        name="dsa",
    )(bound, qa16, qi16, wi, ki16, ka16, va16)
    return out[:, :lq]


def _sb_kernel(qb_ref, kb_ref, vb_ref, tri_ref, o_ref, qbm_ref, tail_ref, acc_ref,
               *, tq, tk, q_off):
    i = pl.program_id(1)
    q0 = q_off + i * tq
    qpos = q0 + lax.broadcasted_iota(jnp.int32, (tq, 1), 0)
    qpos = jnp.concatenate([qpos, qpos], axis=0)
    lane_kpos = lax.broadcasted_iota(jnp.int32, (1, tk), 1)
    tri = tri_ref[...]

    _mask_heads(qb_ref, qbm_ref)
    tail_ref[...] = jnp.zeros(tail_ref.shape, F32)
    acc_ref[...] = jnp.zeros(acc_ref.shape, F32)

    def block(state):
        j, _ = state
        ks = pl.multiple_of(j * tk, tk)
        causal = (ks + lane_kpos) < qpos
        kbb = kb_ref[pl.ds(ks, tk), :]
        vbb = vb_ref[pl.ds(ks, tk), :]
        tail_min = jnp.full((2 * tq, 1), jnp.inf, F32)
        for p in range(N_PAIRS):
            cols = slice(p * PAIR, (p + 1) * PAIR)
            z = lax.dot_general(qbm_ref[p], kbb[:, cols], _NT, preferred_element_type=F32)
            lp = jnp.log(1.0 + jnp.exp(-jnp.abs(z)))
            sp = jnp.where(causal, jnp.maximum(z, 0.0) + lp, 0.0)
            hi = sp.astype(BF16)
            lo = (sp - hi.astype(F32)).astype(BF16)
            inner = (jnp.dot(hi, tri, preferred_element_type=F32)
                     + jnp.dot(lo, tri, preferred_element_type=F32))
            tail_old = tail_ref[p]
            log_a = (jnp.minimum(z, 0.0) - lp) - inner - tail_old[:, :1]
            a = jnp.where(causal, jnp.exp(log_a), 0.0)
            acc_ref[p] = acc_ref[p] + jnp.dot(a.astype(BF16), vbb[:, cols],
                                              preferred_element_type=F32)
            tail_new = tail_old + jnp.sum(sp, axis=1, keepdims=True)
            tail_ref[p] = tail_new
            tail_min = jnp.minimum(tail_min, tail_new[:, :1])
        return j - 1, jnp.min(tail_min)

    def more(state):
        j, tail_min = state
        return (j >= 0) & (tail_min <= SB_EXIT_TAIL)

    lax.while_loop(more, block, ((q0 + tq - 1) // tk, jnp.float32(0.0)))

    lane = lax.broadcasted_iota(jnp.int32, (tq, PAIR), 1)
    for p in range(N_PAIRS):
        acc = acc_ref[p]
        o_ref[:, p * PAIR:(p + 1) * PAIR] = jnp.where(
            lane < HEAD_DIM, acc[:tq], acc[tq:]).astype(BF16)


def _sb(qb16, kb16, vb16, tri, *, tq, tk, q_off):
    b, lq, _ = qb16.shape
    lk = kb16.shape[1]
    qblk = lambda bb, i: (bb, i, 0)
    kblk = lambda bb, i: (bb, 0, 0)
    kern = functools.partial(_sb_kernel, tq=tq, tk=tk, q_off=q_off)
    return pl.pallas_call(
        kern,
        grid=(b, lq // tq),
        in_specs=[pl.BlockSpec((None, tq, WIDTH), qblk),
                  pl.BlockSpec((None, lk, WIDTH), kblk),
                  pl.BlockSpec((None, lk, WIDTH), kblk),
                  _resident((tk, tk), lambda bb, i: (0, 0))],
        out_specs=pl.BlockSpec((None, tq, WIDTH), qblk),
        out_shape=jax.ShapeDtypeStruct((b, lq, WIDTH), BF16),
        scratch_shapes=[pltpu.VMEM((N_PAIRS, 2 * tq, PAIR), BF16),
                        pltpu.VMEM((N_PAIRS, 2 * tq, LANES), F32),
                        pltpu.VMEM((N_PAIRS, 2 * tq, PAIR), F32)],
        compiler_params=pltpu.CompilerParams(dimension_semantics=("arbitrary", "arbitrary"),
                                             vmem_limit_bytes=VMEM_LIMIT),
        name="sb",
    )(qb16, kb16, vb16, tri)


FF_CHUNK = 1024


def _rms_rows(x, g):
    return x * lax.rsqrt(jnp.mean(x * x, axis=-1, keepdims=True) + EPS) * g


def _merge_kernel(x_ref, oa_ref, ob_ref, gmix_ref, gffn_ref, wg_ref, wba_ref, wbb_ref,
                  wout_ref, wup_ref, wdown_ref, y_ref):
    x = x_ref[...]
    hx = _rms_rows(x, gmix_ref[...]).astype(BF16)
    gate_a = jnp.dot(hx, wg_ref[:, :D_MODEL], preferred_element_type=F32)
    gate_b = jnp.dot(hx, wg_ref[:, D_MODEL:], preferred_element_type=F32)
    pa = jnp.dot(oa_ref[...], wba_ref[...], preferred_element_type=F32)
    pb = jnp.dot(ob_ref[...], wbb_ref[...], preferred_element_type=F32)
    m = jax.nn.sigmoid(gate_a) * pa + jax.nn.sigmoid(gate_b) * pb
    h = x + jnp.dot(m.astype(BF16), wout_ref[...], preferred_element_type=F32)
    hn = _rms_rows(h, gffn_ref[...]).astype(BF16)
    y = h
    for c in range(D_FF // FF_CHUNK):
        u = jnp.dot(hn, wup_ref[:, c * FF_CHUNK:(c + 1) * FF_CHUNK], preferred_element_type=F32)
        r = jnp.square(jnp.maximum(u, 0.0)).astype(BF16)
        y = y + jnp.dot(r, wdown_ref[c * FF_CHUNK:(c + 1) * FF_CHUNK, :],
                        preferred_element_type=F32)
    y_ref[...] = y


def _merge_ffn(x2d, oa16, ob16, wts, tm):
    n = x2d.shape[0]
    row = lambda i: (i, 0)
    const = lambda i: (0, 0)
    weights = [wts["w_gate"], wts["w_branch_a"], wts["w_branch_b"], wts["w_out"],
               wts["w_up"], wts["w_down"]]
    return pl.pallas_call(
        _merge_kernel,
        grid=(n // tm,),
        in_specs=[pl.BlockSpec((tm, D_MODEL), row),
                  pl.BlockSpec((tm, WIDTH), row),
                  pl.BlockSpec((tm, WIDTH), row),
                  pl.BlockSpec((1, D_MODEL), const),
                  pl.BlockSpec((1, D_MODEL), const)]
                 + [_resident(w.shape, const) for w in weights],
        out_specs=pl.BlockSpec((tm, D_MODEL), row),
        out_shape=jax.ShapeDtypeStruct((n, D_MODEL), F32),
        compiler_params=pltpu.CompilerParams(dimension_semantics=("arbitrary",),
                                             vmem_limit_bytes=VMEM_LIMIT),
        name="merge_ffn",
    )(x2d, oa16, ob16, wts["g_mix"], wts["g_ffn"], *weights)


def _rope_tables(pos, rows):
    half = HEAD_DIM // 2
    inv_freq = jnp.power(ROPE_THETA, -jnp.arange(half, dtype=F32) / half)
    ang = pos.astype(F32)[:, None] * inv_freq[None, :]
    cos, sin = jnp.cos(ang), jnp.sin(ang)
    cs = jnp.concatenate([cos, cos, cos, cos], axis=1)
    sn = jnp.concatenate([-sin, sin, -sin, sin], axis=1)
    reps = rows // pos.shape[0]
    if reps > 1:
        cs, sn = jnp.tile(cs, (reps, 1)), jnp.tile(sn, (reps, 1))
    return cs, sn


def _prep_weights(g_mix, w_in, g_qn, g_kn, w_branch_a, w_branch_b, w_out, g_ffn, w_up, w_down):
    offs = [int(v) for v in np.cumsum(SPLIT_SIZES)[:-1]]
    w_qa, w_ka, w_va, w_qi, w_ki, w_wi, w_qb, w_kb, w_vb, w_ga, w_gb = jnp.split(w_in, offs, axis=1)
    pad = jnp.zeros((D_MODEL, LANES - IDX_DIM - N_HEADS), w_in.dtype)
    w_proj = jnp.concatenate([w_qa, w_ka, w_va, w_qi, w_qb, w_kb, w_vb, w_ki, w_wi, pad], axis=1)
    head = np.arange(WIDTH) // HEAD_DIM
    return {
        "w_proj": w_proj.astype(BF16),
        "w_gate": jnp.concatenate([w_ga, w_gb], axis=1).astype(BF16),
        "grp": jnp.asarray(head[:, None] == head[None, :], BF16),
        "g_mix": g_mix.reshape(1, D_MODEL),
        "g_ffn": g_ffn.reshape(1, D_MODEL),
        "g_qn": jnp.tile(g_qn, N_HEADS).reshape(1, WIDTH),
        "g_kn": jnp.tile(g_kn, N_HEADS).reshape(1, WIDTH),
        "logit_bound": (HEAD_DIM * QK_SCALE * jnp.max(jnp.abs(g_qn))
                        * jnp.max(jnp.abs(g_kn))).reshape(1).astype(F32),
        "w_branch_a": w_branch_a.astype(BF16),
        "w_branch_b": w_branch_b.astype(BF16),
        "w_out": w_out.astype(BF16),
        "w_up": w_up.astype(BF16),
        "w_down": w_down.astype(BF16),
    }


def _tri(tk):
    r = np.arange(tk)
    return jnp.asarray(r[:, None] > r[None, :], BF16)


def _pad_keys(a, lk):
    return jnp.pad(a, ((0, 0), (0, lk - a.shape[1]), (0, 0)))


def _layer(x, pos, wts, caches, *, tq_dsa, tk_dsa, tq_sb, tk_sb):
    b, l, _ = x.shape
    n = b * l
    tm = min(512, n)
    x2d = x.reshape(n, D_MODEL)
    rows = l if l >= tm else tm
    (ka, va, ki, kb, vb, qa16, ka16, va16, qi16, ki16, qb16, kb16, vb16, wi) = _project(
        x2d, _rope_tables(pos, rows), wts, tm)
    per_seq = lambda a: a.reshape(b, l, a.shape[-1])
    qa16, qi16, qb16, wi = per_seq(qa16), per_seq(qi16), per_seq(qb16), per_seq(wi)
    ka16, va16, ki16, kb16, vb16 = (per_seq(a) for a in (ka16, va16, ki16, kb16, vb16))
    past = 0
    if caches is not None:
        c_ka, c_va, c_ki, c_kb, c_vb = caches
        past = c_ka.shape[1]
        flat = lambda c: c.reshape(b, past, -1).astype(BF16)
        c_ki16 = c_ki.astype(BF16)
        ka16 = jnp.concatenate([flat(c_ka), ka16], axis=1)
        va16 = jnp.concatenate([flat(c_va), va16], axis=1)
        ki16 = jnp.concatenate([jnp.concatenate([c_ki16, c_ki16], axis=2), ki16], axis=1)
        kb16 = jnp.concatenate([flat(c_kb), kb16], axis=1)
        vb16 = jnp.concatenate([flat(c_vb), vb16], axis=1)
    n_valid = past + l
    lk_dsa = pl.cdiv(n_valid, tk_dsa) * tk_dsa
    lk_sb = pl.cdiv(n_valid, tk_sb) * tk_sb
    oa16 = _dsa(wts["logit_bound"], qa16, qi16, wi, _pad_keys(ki16, lk_dsa), _pad_keys(ka16, lk_dsa),
                _pad_keys(va16, lk_dsa), tq=tq_dsa, tk=tk_dsa, n_valid=n_valid, q_off=past)
    ob16 = _sb(qb16, _pad_keys(kb16, lk_sb), _pad_keys(vb16, lk_sb), _tri(tk_sb),
               tq=tq_sb, tk=tk_sb, q_off=past)
    y = _merge_ffn(x2d, oa16.reshape(n, WIDTH), ob16.reshape(n, WIDTH), wts, tm)
    heads = lambda a: a.reshape(b, l, N_HEADS, HEAD_DIM)
    return (y.reshape(b, l, D_MODEL), heads(ka), heads(va), ki.reshape(b, l, IDX_DIM),
            heads(kb), heads(vb))


def kernel(x_prompt, x_sample, cache_k_a, cache_v_a, cache_k_idx, cache_k_sb, cache_v_sb,
           g_mix, w_in, g_qn, g_kn, w_branch_a, w_branch_b, w_out, g_ffn, w_up, w_down):
    wts = _prep_weights(g_mix, w_in, g_qn, g_kn, w_branch_a, w_branch_b, w_out, g_ffn, w_up, w_down)
    s = x_prompt.shape[1]
    t = x_sample.shape[1]
    past = cache_k_a.shape[1]
    prompt = _layer(x_prompt, jnp.arange(s, dtype=jnp.int32), wts, None,
                    tq_dsa=min(256, s), tk_dsa=min(512, s), tq_sb=min(256, s), tk_sb=min(256, s))
    sample = _layer(x_sample, past + jnp.arange(t, dtype=jnp.int32), wts,
                    (cache_k_a, cache_v_a, cache_k_idx, cache_k_sb, cache_v_sb),
                    tq_dsa=LANES, tk_dsa=512, tq_sb=t, tk_sb=128)
    return (prompt[0], sample[0]) + prompt[1:] + sample[1:]
```

```python
import functools

import numpy as np
import jax
import jax.numpy as jnp
from jax import lax
from jax.experimental import pallas as pl
from jax.experimental.pallas import tpu as pltpu

D_MODEL = 1024
HEAD_DIM = 64
N_HEADS = 8
WIDTH = N_HEADS * HEAD_DIM
IDX_DIM = 64
CHUNK = 64
CHUNK_SHIFT = CHUNK.bit_length() - 1
assert 1 << CHUNK_SHIFT == CHUNK
TOPK_MAX = 256
D_FF = 4 * D_MODEL
ROPE_THETA = 10000.0
EPS = 1e-6
SPLIT_SIZES = (WIDTH, WIDTH, WIDTH, N_HEADS * IDX_DIM, IDX_DIM, N_HEADS,
               WIDTH, WIDTH, WIDTH, D_MODEL, D_MODEL)

LANES = 128
PAIR = 2 * HEAD_DIM
N_PAIRS = N_HEADS // 2
QK_SCALE = HEAD_DIM ** -0.5
W_IDX_SCALE = (N_HEADS ** -0.5) * (IDX_DIM ** -0.5)
INT_MIN = -2 ** 31
NEG_BIG = -1e30
SOFTMAX_SAFE_BOUND = 40.0
SB_EXIT_TAIL = 105.0
VMEM_LIMIT = 56 * 1024 * 1024

F32 = jnp.float32
BF16 = jnp.bfloat16
_NT = (((1,), (1,)), ((), ()))


def _resident(shape, index_map):
    return pl.BlockSpec(shape, index_map, pipeline_mode=pl.Buffered(1))


def _per_batch(shape, index_map, steps_per_batch):
    if steps_per_batch == 1:
        return pl.BlockSpec(shape, index_map)
    return _resident(shape, index_map)


def _rope(x, cs, sn):
    w = x.shape[1]
    lane = lax.broadcasted_iota(jnp.int32, x.shape, 1)
    first_half = (lane & (HEAD_DIM // 2)) == 0
    partner = jnp.where(first_half, pltpu.roll(x, w - HEAD_DIM // 2, 1),
                        pltpu.roll(x, HEAD_DIM // 2, 1))
    reps = w // LANES
    if reps > 1:
        cs = jnp.concatenate([cs] * reps, axis=1)
        sn = jnp.concatenate([sn] * reps, axis=1)
    return x * cs + partner * sn


def _head_rmsnorm(z, grp, g):
    zz = z * z
    hi = zz.astype(BF16)
    lo = (zz - hi.astype(F32)).astype(BF16)
    ss = (jnp.dot(hi, grp, preferred_element_type=F32)
          + jnp.dot(lo, grp, preferred_element_type=F32))
    return z * lax.rsqrt(ss * (1.0 / HEAD_DIM) + EPS) * g


def _proj_kernel(x_ref, gmix_ref, w_ref, gqn_ref, gkn_ref, cs_ref, sn_ref, grp_ref,
                 ka_ref, va_ref, ki_ref, kb_ref, vb_ref,
                 qa16_ref, ka16_ref, va16_ref, qi16_ref, ki16_ref,
                 qb16_ref, kb16_ref, vb16_ref, wi_ref):
    x = x_ref[...]
    h = (x * lax.rsqrt(jnp.mean(x * x, axis=-1, keepdims=True) + EPS)
         * gmix_ref[...]).astype(BF16)
    cs = cs_ref[...]
    sn = sn_ref[...]
    grp = grp_ref[...]

    def seg(k, width=WIDTH):
        return jnp.dot(h, w_ref[:, k * WIDTH:k * WIDTH + width],
                       preferred_element_type=F32)

    split_heads = lambda z: pltpu.einshape("m(hd)->mhd", z, h=N_HEADS)

    qa = _rope(_head_rmsnorm(seg(0), grp, gqn_ref[...]), cs, sn)
    qa16_ref[...] = (qa * QK_SCALE).astype(BF16)
    ka = _rope(_head_rmsnorm(seg(1), grp, gkn_ref[...]), cs, sn)
    ka_ref[...] = split_heads(ka)
    ka16_ref[...] = ka.astype(BF16)
    va = seg(2)
    va_ref[...] = split_heads(va)
    va16_ref[...] = va.astype(BF16)
    qi16_ref[...] = _rope(seg(3), cs, sn).astype(BF16)
    qb16_ref[...] = (seg(4) * QK_SCALE).astype(BF16)
    kb = seg(5)
    kb_ref[...] = split_heads(kb)
    kb16_ref[...] = kb.astype(BF16)
    vb = seg(6)
    vb_ref[...] = split_heads(vb)
    vb16_ref[...] = vb.astype(BF16)
    misc = seg(7, LANES)
    ki = _rope(misc, cs, sn)[:, :IDX_DIM]
    ki_ref[...] = ki
    ki16 = ki.astype(BF16)
    ki16_ref[...] = jnp.concatenate([ki16, ki16], axis=1)
    wi_ref[...] = misc * W_IDX_SCALE


def _project(x2d, tabs, wts, tm):
    n = x2d.shape[0]
    cs, sn = tabs
    nt = cs.shape[0] // tm
    row = lambda i: (i, 0)
    const = lambda i: (0, 0)
    tab = lambda i: (i % nt, 0)
    f32_leaf = lambda w: jax.ShapeDtypeStruct((n, w), F32)
    heads_leaf = jax.ShapeDtypeStruct((n, N_HEADS, HEAD_DIM), F32)
    b16 = lambda w: jax.ShapeDtypeStruct((n, w), BF16)
    out_shape = [heads_leaf, heads_leaf, f32_leaf(IDX_DIM), heads_leaf, heads_leaf,
                 b16(WIDTH), b16(WIDTH), b16(WIDTH), b16(WIDTH), b16(LANES),
                 b16(WIDTH), b16(WIDTH), b16(WIDTH), f32_leaf(LANES)]
    out_specs = [pl.BlockSpec((tm,) + s.shape[1:], lambda i, nd=len(s.shape): (i,) + (0,) * (nd - 1))
                 for s in out_shape]
    w_proj = wts["w_proj"]
    return pl.pallas_call(
        _proj_kernel,
        grid=(n // tm,),
        in_specs=[pl.BlockSpec((tm, D_MODEL), row),
                  pl.BlockSpec((1, D_MODEL), const),
                  _resident(w_proj.shape, const),
                  pl.BlockSpec((1, WIDTH), const),
                  pl.BlockSpec((1, WIDTH), const),
                  pl.BlockSpec((tm, LANES), tab),
                  pl.BlockSpec((tm, LANES), tab),
                  _resident((WIDTH, WIDTH), const)],
        out_specs=out_specs,
        out_shape=out_shape,
        compiler_params=pltpu.CompilerParams(dimension_semantics=("arbitrary",),
                                             vmem_limit_bytes=VMEM_LIMIT),
        name="proj",
    )(x2d, wts["g_mix"], w_proj, wts["g_qn"], wts["g_kn"], cs, sn, wts["grp"])


SUBLANES = 8
DSA_GROUP = 4
SB_GROUP = 4
COUNT_CHAINS = 4


HALF_MIN = -2 ** 15


def _key_to_float(k):
    return lax.bitcast_convert_type(k ^ ((k >> 31) & 0x7FFFFFFF), F32)


def _mask_heads(src_ref, dst_ref, group=2):
    tq = src_ref.shape[0]
    gw = group * HEAD_DIM
    head = lax.broadcasted_iota(jnp.int32, (tq, gw), 1) // HEAD_DIM
    for g in range(N_HEADS // group):
        cols = src_ref[:, g * gw:(g + 1) * gw]
        zero = jnp.zeros_like(cols)
        for h in range(group):
            dst_ref[g, h * tq:(h + 1) * tq, :] = jnp.where(head == h, cols, zero)


def _lanes(x, reps):
    return jnp.concatenate([x] * reps, axis=1) if reps > 1 else x


def _dsa_kernel(bound_ref, qa_ref, qi_ref, wi_ref, ki_ref, ka_ref, va_ref, o_ref,
                sc_ref, hi_ref, lo_ref, qam_ref, qim_ref, w_ref, shift_ref, l_ref, acc_ref,
                thr_ref, cut_ref, *, tq, tk, n_valid, q_off, topk, group):
    i = pl.program_id(1)
    q0 = q_off + i * tq
    qpos = q0 + lax.broadcasted_iota(jnp.int32, (1, tq), 1)
    chunk_end = lambda pos: ((pos >> CHUNK_SHIFT) + 1) << CHUNK_SHIFT
    lim = jnp.minimum(chunk_end(qpos), n_valid)
    lim16 = lim.astype(jnp.int16)
    n_adm = jnp.minimum(chunk_end(q0 + tq - 1), n_valid)
    nkb = (n_adm + tk - 1) // tk
    reps = tk // LANES
    gw = group * HEAD_DIM
    n_groups = N_HEADS // group
    key_iota = lax.broadcasted_iota(jnp.int32, (tk, 1), 0)

    _mask_heads(qi_ref, qim_ref, group)
    _mask_heads(qa_ref, qam_ref, group)
    w_t = wi_ref[...].T
    for g in range(n_groups):
        r0 = IDX_DIM + group * g
        row = jnp.concatenate([w_t[r0 + h:r0 + h + 1] for h in range(group)], axis=1)
        w_ref[g] = jnp.broadcast_to(row, (SUBLANES, group * tq))

    def score_block(j, carry):
        ks = pl.multiple_of(j * tk, tk)
        kib = _lanes(ki_ref[pl.ds(ks, tk), :], gw // LANES)
        score = jnp.zeros((tk, tq), F32)
        for g in range(n_groups):
            s = lax.dot_general(kib, qim_ref[g], _NT, preferred_element_type=F32)
            r = jnp.maximum(s, 0.0) * w_ref[g][:1]
            for h in range(group):
                score = score + r[:, h * tq:(h + 1) * tq]
        sc_ref[pl.ds(ks, tk), :] = jnp.where(ks + key_iota < lim, score, -jnp.inf)
        bits = lax.bitcast_convert_type(score, jnp.int32)
        hi = (bits >> 16).astype(jnp.int16)
        lo = bits.astype(jnp.int16)
        sign = jnp.where(hi < 0, jnp.int16(-1), jnp.int16(0))
        adm = (ks + key_iota).astype(jnp.int16) < lim16
        hi_ref[pl.ds(ks, tk), :] = jnp.where(adm, hi ^ (sign & jnp.int16(0x7FFF)),
                                             jnp.int16(HALF_MIN))
        lo_ref[pl.ds(ks, tk), :] = jnp.where(adm, lo ^ sign ^ jnp.int16(HALF_MIN),
                                             jnp.int16(HALF_MIN))
        return carry

    lax.fori_loop(0, nkb, score_block, 0)

    def count(src_ref, pred):
        dt = src_ref.dtype
        sub = SUBLANES * 4 // jnp.dtype(dt).itemsize
        one, zero = jnp.ones((), dt), jnp.zeros((), dt)

        def body(c, acc):
            ks = pl.multiple_of(c * tk, tk)
            blk = src_ref[pl.ds(ks, tk), :]
            parts = [jnp.zeros((sub, tq), dt) for _ in range(COUNT_CHAINS)]
            for r in range(tk // sub):
                hit = pred(blk[r * sub:(r + 1) * sub], ks + r * sub)
                parts[r % COUNT_CHAINS] = parts[r % COUNT_CHAINS] + jnp.where(hit, one, zero)
            part = functools.reduce(lambda a, b: a + b, parts).astype(F32)
            for half in range(sub // SUBLANES - 1):
                part = part[:SUBLANES] + part[SUBLANES:]
            return acc + part

        acc = lax.fori_loop(0, nkb, body, jnp.zeros((SUBLANES, tq), F32))
        return jnp.broadcast_to(jnp.sum(acc, axis=0, keepdims=True), (SUBLANES, tq))

    def halves(v):
        u = v.astype(jnp.int16)
        return jnp.concatenate([u, u], axis=0)

    def largest_half(src_ref, wanted):
        nonneg = count(src_ref, lambda blk, k0: blk >= jnp.int16(0)) >= wanted
        h0 = jnp.where(nonneg, jnp.zeros((SUBLANES, tq), jnp.int32),
                       jnp.full((SUBLANES, tq), HALF_MIN, jnp.int32))

        def bit_step(t, h):
            cand = h | (jnp.int32(1) << (14 - t))
            cand16 = halves(cand)
            return jnp.where(count(src_ref, lambda blk, k0: blk >= cand16) >= wanted, cand, h)

        return lax.fori_loop(0, 15, bit_step, h0)

    kf = float(topk)
    hi = largest_half(hi_ref, kf)
    hi16 = halves(hi)
    wanted_lo = kf - count(hi_ref, lambda blk, k0: blk > hi16)

    def keep_bucket(c, carry):
        ks = pl.multiple_of(c * tk, tk)
        in_bucket = hi_ref[pl.ds(ks, tk), :] == hi16[:1]
        lo_ref[pl.ds(ks, tk), :] = jnp.where(in_bucket, lo_ref[pl.ds(ks, tk), :],
                                             jnp.int16(HALF_MIN))
        return carry

    lax.fori_loop(0, nkb, keep_bucket, 0)
    lo = largest_half(lo_ref, wanted_lo)
    key = (hi << 16) | ((lo - HALF_MIN) & 0xFFFF)
    has_thr = key != INT_MIN
    thr = jnp.where(has_thr, _key_to_float(key), -jnp.inf)
    thr_ref[...] = thr
    need = kf - count(sc_ref, lambda blk, k0: blk > thr)
    c_eq = count(sc_ref, lambda blk, k0: blk == thr)
    cut_ref[...] = jnp.where(has_thr, jnp.int32(2 ** 30), jnp.int32(-1))
    split = has_thr & (c_eq > need)

    @pl.when(jnp.max(jnp.where(split, 1.0, 0.0)) > 0.0)
    def _():
        sub_iota = lax.broadcasted_iota(jnp.int32, (SUBLANES, 1), 0)

        def idx_step(t, cut):
            cand = cut | (jnp.int32(1) << (13 - t))
            below = count(sc_ref, lambda blk, k0: (blk == thr) & (k0 + sub_iota < cand))
            return jnp.where(below <= need - 1.0, cand, cut)

        cut = lax.fori_loop(0, 14, idx_step, jnp.zeros((SUBLANES, tq), jnp.int32))
        cut_ref[...] = jnp.where(has_thr, cut, -1)

    def masked_bias(ks):
        scb = sc_ref[pl.ds(ks, tk), :]
        thr_b = thr_ref[:1, :]
        sel = (scb > thr_b) | ((scb == thr_b) & (ks + key_iota <= cut_ref[:1, :]))
        bias = jnp.where(sel, 0.0, NEG_BIG).T
        return jnp.concatenate([bias] * group, axis=0)

    def logits(g, kab):
        return lax.dot_general(qam_ref[g], kab[:, g * gw:(g + 1) * gw], _NT,
                               preferred_element_type=F32)

    bound = bound_ref[0]
    shift_ref[...] = jnp.full(shift_ref.shape, bound, F32)

    @pl.when(bound > SOFTMAX_SAFE_BOUND)
    def _():
        l_ref[...] = jnp.full(l_ref.shape, NEG_BIG, F32)

        def max_block(j, carry):
            ks = pl.multiple_of(j * tk, tk)
            bias = masked_bias(ks)
            kab = ka_ref[pl.ds(ks, tk), :]
            for g in range(n_groups):
                s = logits(g, kab) + bias
                mx = l_ref[g]
                for u in range(reps):
                    mx = jnp.maximum(mx, s[:, u * LANES:(u + 1) * LANES])
                l_ref[g] = mx
            return carry

        lax.fori_loop(0, nkb, max_block, 0)
        for g in range(n_groups):
            shift_ref[g] = jnp.broadcast_to(jnp.max(l_ref[g], axis=1, keepdims=True),
                                            (group * tq, LANES))

    l_ref[...] = jnp.zeros(l_ref.shape, F32)
    acc_ref[...] = jnp.zeros(acc_ref.shape, F32)

    def attend_block(j, carry):
        ks = pl.multiple_of(j * tk, tk)
        bias = masked_bias(ks)
        kab = ka_ref[pl.ds(ks, tk), :]
        vab = va_ref[pl.ds(ks, tk), :]
        for g in range(n_groups):
            e = jnp.exp((logits(g, kab) - _lanes(shift_ref[g], reps)) + bias)
            part = l_ref[g]
            for u in range(reps):
                part = part + e[:, u * LANES:(u + 1) * LANES]
            l_ref[g] = part
            acc_ref[g] = acc_ref[g] + jnp.dot(
                e.astype(BF16), vab[:, g * gw:(g + 1) * gw], preferred_element_type=F32)
        return carry

    lax.fori_loop(0, nkb, attend_block, 0)

    head_o = lax.broadcasted_iota(jnp.int32, (tq, gw), 1) // HEAD_DIM
    for g in range(n_groups):
        out = acc_ref[g] / jnp.sum(l_ref[g], axis=1, keepdims=True)
        res = out[:tq]
        for h in range(1, group):
            res = jnp.where(head_o == h, out[h * tq:(h + 1) * tq], res)
        o_ref[:, g * gw:(g + 1) * gw] = res.astype(BF16)


def _dsa(bound, qa16, qi16, wi, ki16, ka16, va16, *, tq, tk, n_valid, q_off, group=DSA_GROUP):
    b, lq, _ = qa16.shape
    lk = ka16.shape[1]
    topk = min(TOPK_MAX, n_valid // 4)
    lq_pad = pl.cdiv(lq, LANES) * LANES
    if lq_pad != lq:
        pad = lambda a: jnp.pad(a, ((0, 0), (0, lq_pad - lq), (0, 0)))
        qa16, qi16, wi = pad(qa16), pad(qi16), pad(wi)
    tq = min(tq, lq_pad)
    assert tq % LANES == 0 and lq_pad % tq == 0 and lk % tk == 0
    assert lk < -HALF_MIN, "key indices are compared as int16"
    qblk = lambda bb, i: (bb, i, 0)
    kblk = lambda bb, i: (bb, 0, 0)
    kern = functools.partial(_dsa_kernel, tq=tq, tk=tk, n_valid=n_valid, q_off=q_off, topk=topk,
                             group=group)
    n_groups, gw = N_HEADS // group, group * HEAD_DIM
    stacked = lambda w, dt: pltpu.VMEM((n_groups, group * tq, w), dt)
    out = pl.pallas_call(
        kern,
        grid=(b, lq_pad // tq),
        in_specs=[pl.BlockSpec(memory_space=pltpu.SMEM),
                  pl.BlockSpec((None, tq, WIDTH), qblk),
                  pl.BlockSpec((None, tq, WIDTH), qblk),
                  pl.BlockSpec((None, tq, LANES), qblk),
                  _per_batch((None, lk, LANES), kblk, lq_pad // tq),
                  _per_batch((None, lk, WIDTH), kblk, lq_pad // tq),
                  _per_batch((None, lk, WIDTH), kblk, lq_pad // tq)],
        out_specs=pl.BlockSpec((None, tq, WIDTH), qblk),
        out_shape=jax.ShapeDtypeStruct((b, lq_pad, WIDTH), BF16),
        scratch_shapes=[pltpu.VMEM((lk, tq), F32),
                        pltpu.VMEM((lk, tq), jnp.int16),
                        pltpu.VMEM((lk, tq), jnp.int16),
                        stacked(gw, BF16),
                        stacked(gw, BF16),
                        pltpu.VMEM((n_groups, SUBLANES, group * tq), F32),
                        stacked(LANES, F32),
                        stacked(LANES, F32),
                        stacked(gw, F32),
                        pltpu.VMEM((SUBLANES, tq), F32),
                        pltpu.VMEM((SUBLANES, tq), jnp.int32)],
        compiler_params=pltpu.CompilerParams(dimension_semantics=("arbitrary", "arbitrary"),
                                             vmem_limit_bytes=VMEM_LIMIT),
        name="dsa",
    )(bound, qa16, qi16, wi, ki16, ka16, va16)
    return out[:, :lq]


def _sb_kernel(qb_ref, kb_ref, vb_ref, tri_ref, o_ref, qbm_ref, tail_ref, acc_ref,
               *, tq, tk, q_off, group):
    i = pl.program_id(1)
    q0 = q_off + i * tq
    gw = group * HEAD_DIM
    n_groups = N_HEADS // group
    qpos = q0 + lax.broadcasted_iota(jnp.int32, (tq, 1), 0)
    qpos = jnp.concatenate([qpos] * group, axis=0)
    lane_kpos = lax.broadcasted_iota(jnp.int32, (1, tk), 1)
    tri = tri_ref[...]

    _mask_heads(qb_ref, qbm_ref, group)
    tail_ref[...] = jnp.zeros(tail_ref.shape, F32)
    acc_ref[...] = jnp.zeros(acc_ref.shape, F32)

    def block(state):
        j, _ = state
        ks = pl.multiple_of(j * tk, tk)
        causal = (ks + lane_kpos) < qpos
        kbb = kb_ref[pl.ds(ks, tk), :]
        vbb = vb_ref[pl.ds(ks, tk), :]
        tail_min = jnp.full((group * tq, 1), jnp.inf, F32)
        for p in range(n_groups):
            cols = slice(p * gw, (p + 1) * gw)
            z = lax.dot_general(qbm_ref[p], kbb[:, cols], _NT, preferred_element_type=F32)
            lp = jnp.log(1.0 + jnp.exp(-jnp.abs(z)))
            sp = jnp.where(causal, jnp.maximum(z, 0.0) + lp, 0.0)
            hi = sp.astype(BF16)
            lo = (sp - hi.astype(F32)).astype(BF16)
            inner = (jnp.dot(hi, tri, preferred_element_type=F32)
                     + jnp.dot(lo, tri, preferred_element_type=F32))
            tail_old = tail_ref[p]
            log_a = (jnp.minimum(z, 0.0) - lp) - inner - tail_old[:, :1]
            a = jnp.where(causal, jnp.exp(log_a), 0.0)
            acc_ref[p] = acc_ref[p] + jnp.dot(a.astype(BF16), vbb[:, cols],
                                              preferred_element_type=F32)
            tail_new = tail_old + jnp.sum(sp, axis=1, keepdims=True)
            tail_ref[p] = tail_new
            tail_min = jnp.minimum(tail_min, tail_new[:, :1])
        return j - 1, jnp.min(tail_min)

    def more(state):
        j, tail_min = state
        return (j >= 0) & (tail_min <= SB_EXIT_TAIL)

    lax.while_loop(more, block, ((q0 + tq - 1) // tk, jnp.float32(0.0)))

    head_o = lax.broadcasted_iota(jnp.int32, (tq, gw), 1) // HEAD_DIM
    for p in range(n_groups):
        acc = acc_ref[p]
        res = acc[:tq]
        for h in range(1, group):
            res = jnp.where(head_o == h, acc[h * tq:(h + 1) * tq], res)
        o_ref[:, p * gw:(p + 1) * gw] = res.astype(BF16)


def _sb(qb16, kb16, vb16, tri, *, tq, tk, q_off, group=SB_GROUP):
    b, lq, _ = qb16.shape
    lk = kb16.shape[1]
    qblk = lambda bb, i: (bb, i, 0)
    kblk = lambda bb, i: (bb, 0, 0)
    kern = functools.partial(_sb_kernel, tq=tq, tk=tk, q_off=q_off, group=group)
    n_groups, gw = N_HEADS // group, group * HEAD_DIM
    return pl.pallas_call(
        kern,
        grid=(b, lq // tq),
        in_specs=[pl.BlockSpec((None, tq, WIDTH), qblk),
                  pl.BlockSpec((None, lk, WIDTH), kblk),
                  pl.BlockSpec((None, lk, WIDTH), kblk),
                  _resident((tk, tk), lambda bb, i: (0, 0))],
        out_specs=pl.BlockSpec((None, tq, WIDTH), qblk),
        out_shape=jax.ShapeDtypeStruct((b, lq, WIDTH), BF16),
        scratch_shapes=[pltpu.VMEM((n_groups, group * tq, gw), BF16),
                        pltpu.VMEM((n_groups, group * tq, LANES), F32),
                        pltpu.VMEM((n_groups, group * tq, gw), F32)],
        compiler_params=pltpu.CompilerParams(dimension_semantics=("arbitrary", "arbitrary"),
                                             vmem_limit_bytes=VMEM_LIMIT),
        name="sb",
    )(qb16, kb16, vb16, tri)


FF_CHUNK = 1024


def _rms_rows(x, g):
    return x * lax.rsqrt(jnp.mean(x * x, axis=-1, keepdims=True) + EPS) * g


def _merge_kernel(x_ref, oa_ref, ob_ref, gmix_ref, gffn_ref, wg_ref, wba_ref, wbb_ref,
                  wout_ref, wup_ref, wdown_ref, y_ref):
    x = x_ref[...]
    hx = _rms_rows(x, gmix_ref[...]).astype(BF16)
    gate_a = jnp.dot(hx, wg_ref[:, :D_MODEL], preferred_element_type=F32)
    gate_b = jnp.dot(hx, wg_ref[:, D_MODEL:], preferred_element_type=F32)
    pa = jnp.dot(oa_ref[...], wba_ref[...], preferred_element_type=F32)
    pb = jnp.dot(ob_ref[...], wbb_ref[...], preferred_element_type=F32)
    m = jax.nn.sigmoid(gate_a) * pa + jax.nn.sigmoid(gate_b) * pb
    h = x + jnp.dot(m.astype(BF16), wout_ref[...], preferred_element_type=F32)
    hn = _rms_rows(h, gffn_ref[...]).astype(BF16)
    y = h
    for c in range(D_FF // FF_CHUNK):
        u = jnp.dot(hn, wup_ref[:, c * FF_CHUNK:(c + 1) * FF_CHUNK], preferred_element_type=F32)
        r = jnp.square(jnp.maximum(u, 0.0)).astype(BF16)
        y = y + jnp.dot(r, wdown_ref[c * FF_CHUNK:(c + 1) * FF_CHUNK, :],
                        preferred_element_type=F32)
    y_ref[...] = y


def _merge_ffn(x2d, oa16, ob16, wts, tm):
    n = x2d.shape[0]
    row = lambda i: (i, 0)
    const = lambda i: (0, 0)
    weights = [wts["w_gate"], wts["w_branch_a"], wts["w_branch_b"], wts["w_out"],
               wts["w_up"], wts["w_down"]]
    return pl.pallas_call(
        _merge_kernel,
        grid=(n // tm,),
        in_specs=[pl.BlockSpec((tm, D_MODEL), row),
                  pl.BlockSpec((tm, WIDTH), row),
                  pl.BlockSpec((tm, WIDTH), row),
                  pl.BlockSpec((1, D_MODEL), const),
                  pl.BlockSpec((1, D_MODEL), const)]
                 + [_resident(w.shape, const) for w in weights],
        out_specs=pl.BlockSpec((tm, D_MODEL), row),
        out_shape=jax.ShapeDtypeStruct((n, D_MODEL), F32),
        compiler_params=pltpu.CompilerParams(dimension_semantics=("arbitrary",),
                                             vmem_limit_bytes=VMEM_LIMIT),
        name="merge_ffn",
    )(x2d, oa16, ob16, wts["g_mix"], wts["g_ffn"], *weights)


def _rope_tables(pos, rows):
    half = HEAD_DIM // 2
    inv_freq = jnp.power(ROPE_THETA, -jnp.arange(half, dtype=F32) / half)
    ang = pos.astype(F32)[:, None] * inv_freq[None, :]
    cos, sin = jnp.cos(ang), jnp.sin(ang)
    cs = jnp.concatenate([cos, cos, cos, cos], axis=1)
    sn = jnp.concatenate([-sin, sin, -sin, sin], axis=1)
    reps = rows // pos.shape[0]
    if reps > 1:
        cs, sn = jnp.tile(cs, (reps, 1)), jnp.tile(sn, (reps, 1))
    return cs, sn


def _prep_weights(g_mix, w_in, g_qn, g_kn, w_branch_a, w_branch_b, w_out, g_ffn, w_up, w_down):
    offs = [int(v) for v in np.cumsum(SPLIT_SIZES)[:-1]]
    w_qa, w_ka, w_va, w_qi, w_ki, w_wi, w_qb, w_kb, w_vb, w_ga, w_gb = jnp.split(w_in, offs, axis=1)
    pad = jnp.zeros((D_MODEL, LANES - IDX_DIM - N_HEADS), w_in.dtype)
    w_proj = jnp.concatenate([w_qa, w_ka, w_va, w_qi, w_qb, w_kb, w_vb, w_ki, w_wi, pad], axis=1)
    head = np.arange(WIDTH) // HEAD_DIM
    return {
        "w_proj": w_proj.astype(BF16),
        "w_gate": jnp.concatenate([w_ga, w_gb], axis=1).astype(BF16),
        "grp": jnp.asarray(head[:, None] == head[None, :], BF16),
        "g_mix": g_mix.reshape(1, D_MODEL),
        "g_ffn": g_ffn.reshape(1, D_MODEL),
        "g_qn": jnp.tile(g_qn, N_HEADS).reshape(1, WIDTH),
        "g_kn": jnp.tile(g_kn, N_HEADS).reshape(1, WIDTH),
        "logit_bound": (HEAD_DIM * QK_SCALE * jnp.max(jnp.abs(g_qn))
                        * jnp.max(jnp.abs(g_kn))).reshape(1).astype(F32),
        "w_branch_a": w_branch_a.astype(BF16),
        "w_branch_b": w_branch_b.astype(BF16),
        "w_out": w_out.astype(BF16),
        "w_up": w_up.astype(BF16),
        "w_down": w_down.astype(BF16),
    }


def _tri(tk):
    r = np.arange(tk)
    return jnp.asarray(r[:, None] > r[None, :], BF16)


def _pad_keys(a, lk):
    return jnp.pad(a, ((0, 0), (0, lk - a.shape[1]), (0, 0)))


def _layer(x, pos, wts, caches, *, tq_dsa, tk_dsa, tq_sb, tk_sb):
    b, l, _ = x.shape
    n = b * l
    tm = min(512, n)
    x2d = x.reshape(n, D_MODEL)
    rows = l if l >= tm else tm
    (ka, va, ki, kb, vb, qa16, ka16, va16, qi16, ki16, qb16, kb16, vb16, wi) = _project(
        x2d, _rope_tables(pos, rows), wts, tm)
    per_seq = lambda a: a.reshape(b, l, a.shape[-1])
    qa16, qi16, qb16, wi = per_seq(qa16), per_seq(qi16), per_seq(qb16), per_seq(wi)
    ka16, va16, ki16, kb16, vb16 = (per_seq(a) for a in (ka16, va16, ki16, kb16, vb16))
    past = 0
    if caches is not None:
        c_ka, c_va, c_ki, c_kb, c_vb = caches
        past = c_ka.shape[1]
        flat = lambda c: c.reshape(b, past, -1).astype(BF16)
        c_ki16 = c_ki.astype(BF16)
        ka16 = jnp.concatenate([flat(c_ka), ka16], axis=1)
        va16 = jnp.concatenate([flat(c_va), va16], axis=1)
        ki16 = jnp.concatenate([jnp.concatenate([c_ki16, c_ki16], axis=2), ki16], axis=1)
        kb16 = jnp.concatenate([flat(c_kb), kb16], axis=1)
        vb16 = jnp.concatenate([flat(c_vb), vb16], axis=1)
    n_valid = past + l
    lk_dsa = pl.cdiv(n_valid, tk_dsa) * tk_dsa
    lk_sb = pl.cdiv(n_valid, tk_sb) * tk_sb
    oa16 = _dsa(wts["logit_bound"], qa16, qi16, wi, _pad_keys(ki16, lk_dsa), _pad_keys(ka16, lk_dsa),
                _pad_keys(va16, lk_dsa), tq=tq_dsa, tk=tk_dsa, n_valid=n_valid, q_off=past)
    ob16 = _sb(qb16, _pad_keys(kb16, lk_sb), _pad_keys(vb16, lk_sb), _tri(tk_sb),
               tq=tq_sb, tk=tk_sb, q_off=past)
    y = _merge_ffn(x2d, oa16.reshape(n, WIDTH), ob16.reshape(n, WIDTH), wts, tm)
    heads = lambda a: a.reshape(b, l, N_HEADS, HEAD_DIM)
    return (y.reshape(b, l, D_MODEL), heads(ka), heads(va), ki.reshape(b, l, IDX_DIM),
            heads(kb), heads(vb))


def kernel(x_prompt, x_sample, cache_k_a, cache_v_a, cache_k_idx, cache_k_sb, cache_v_sb,
           g_mix, w_in, g_qn, g_kn, w_branch_a, w_branch_b, w_out, g_ffn, w_up, w_down):
    wts = _prep_weights(g_mix, w_in, g_qn, g_kn, w_branch_a, w_branch_b, w_out, g_ffn, w_up, w_down)
    s = x_prompt.shape[1]
    t = x_sample.shape[1]
    past = cache_k_a.shape[1]
    prompt = _layer(x_prompt, jnp.arange(s, dtype=jnp.int32), wts, None,
                    tq_dsa=min(256, s), tk_dsa=min(1024, s), tq_sb=min(256, s), tk_sb=min(256, s))
    sample = _layer(x_sample, past + jnp.arange(t, dtype=jnp.int32), wts,
                    (cache_k_a, cache_v_a, cache_k_idx, cache_k_sb, cache_v_sb),
                    tq_dsa=LANES, tk_dsa=512, tq_sb=t, tk_sb=128)
    return (prompt[0], sample[0]) + prompt[1:] + sample[1:]
```

```python
import functools

import numpy as np
import jax
import jax.numpy as jnp
from jax import lax
from jax.experimental import pallas as pl
from jax.experimental.pallas import tpu as pltpu

D_MODEL = 1024
HEAD_DIM = 64
N_HEADS = 8
WIDTH = N_HEADS * HEAD_DIM
IDX_DIM = 64
CHUNK = 64
CHUNK_SHIFT = CHUNK.bit_length() - 1
assert 1 << CHUNK_SHIFT == CHUNK
TOPK_MAX = 256
D_FF = 4 * D_MODEL
ROPE_THETA = 10000.0
EPS = 1e-6
SPLIT_SIZES = (WIDTH, WIDTH, WIDTH, N_HEADS * IDX_DIM, IDX_DIM, N_HEADS,
               WIDTH, WIDTH, WIDTH, D_MODEL, D_MODEL)

LANES = 128
PAIR = 2 * HEAD_DIM
N_PAIRS = N_HEADS // 2
QK_SCALE = HEAD_DIM ** -0.5
W_IDX_SCALE = (N_HEADS ** -0.5) * (IDX_DIM ** -0.5)
INT_MIN = -2 ** 31
NEG_BIG = -1e30
SOFTMAX_SAFE_BOUND = 40.0
SB_EXIT_TAIL = 105.0
VMEM_LIMIT = 56 * 1024 * 1024

F32 = jnp.float32
BF16 = jnp.bfloat16
_NT = (((1,), (1,)), ((), ()))


def _resident(shape, index_map):
    return pl.BlockSpec(shape, index_map, pipeline_mode=pl.Buffered(1))


def _per_batch(shape, index_map, steps_per_batch):
    if steps_per_batch == 1:
        return pl.BlockSpec(shape, index_map)
    return _resident(shape, index_map)


def _rope(x, cs, sn):
    w = x.shape[1]
    lane = lax.broadcasted_iota(jnp.int32, x.shape, 1)
    first_half = (lane & (HEAD_DIM // 2)) == 0
    partner = jnp.where(first_half, pltpu.roll(x, w - HEAD_DIM // 2, 1),
                        pltpu.roll(x, HEAD_DIM // 2, 1))
    reps = w // LANES
    if reps > 1:
        cs = jnp.concatenate([cs] * reps, axis=1)
        sn = jnp.concatenate([sn] * reps, axis=1)
    return x * cs + partner * sn


def _head_rmsnorm(z, grp, g):
    zz = z * z
    hi = zz.astype(BF16)
    lo = (zz - hi.astype(F32)).astype(BF16)
    ss = (jnp.dot(hi, grp, preferred_element_type=F32)
          + jnp.dot(lo, grp, preferred_element_type=F32))
    return z * lax.rsqrt(ss * (1.0 / HEAD_DIM) + EPS) * g


def _proj_kernel(x_ref, gmix_ref, w_ref, gqn_ref, gkn_ref, cs_ref, sn_ref, grp_ref,
                 ka_ref, va_ref, ki_ref, kb_ref, vb_ref,
                 qa16_ref, ka16_ref, va16_ref, qi16_ref, ki16_ref,
                 qb16_ref, kb16_ref, vb16_ref, wi_ref):
    x = x_ref[...]
    h = (x * lax.rsqrt(jnp.mean(x * x, axis=-1, keepdims=True) + EPS)
         * gmix_ref[...]).astype(BF16)
    cs = cs_ref[...]
    sn = sn_ref[...]
    grp = grp_ref[...]

    def seg(k, width=WIDTH):
        return jnp.dot(h, w_ref[:, k * WIDTH:k * WIDTH + width],
                       preferred_element_type=F32)

    split_heads = lambda z: pltpu.einshape("m(hd)->mhd", z, h=N_HEADS)

    qa = _rope(_head_rmsnorm(seg(0), grp, gqn_ref[...]), cs, sn)
    qa16_ref[...] = (qa * QK_SCALE).astype(BF16)
    ka = _rope(_head_rmsnorm(seg(1), grp, gkn_ref[...]), cs, sn)
    ka_ref[...] = split_heads(ka)
    ka16_ref[...] = ka.astype(BF16)
    va = seg(2)
    va_ref[...] = split_heads(va)
    va16_ref[...] = va.astype(BF16)
    qi16_ref[...] = _rope(seg(3), cs, sn).astype(BF16)
    qb16_ref[...] = (seg(4) * QK_SCALE).astype(BF16)
    kb = seg(5)
    kb_ref[...] = split_heads(kb)
    kb16_ref[...] = kb.astype(BF16)
    vb = seg(6)
    vb_ref[...] = split_heads(vb)
    vb16_ref[...] = vb.astype(BF16)
    misc = seg(7, LANES)
    ki = _rope(misc, cs, sn)[:, :IDX_DIM]
    ki_ref[...] = ki
    ki16 = ki.astype(BF16)
    ki16_ref[...] = jnp.concatenate([ki16, ki16], axis=1)
    wi_ref[...] = misc * W_IDX_SCALE


def _project(x2d, tabs, wts, tm):
    n = x2d.shape[0]
    cs, sn = tabs
    nt = cs.shape[0] // tm
    row = lambda i: (i, 0)
    const = lambda i: (0, 0)
    tab = lambda i: (i % nt, 0)
    f32_leaf = lambda w: jax.ShapeDtypeStruct((n, w), F32)
    heads_leaf = jax.ShapeDtypeStruct((n, N_HEADS, HEAD_DIM), F32)
    b16 = lambda w: jax.ShapeDtypeStruct((n, w), BF16)
    out_shape = [heads_leaf, heads_leaf, f32_leaf(IDX_DIM), heads_leaf, heads_leaf,
                 b16(WIDTH), b16(WIDTH), b16(WIDTH), b16(WIDTH), b16(LANES),
                 b16(WIDTH), b16(WIDTH), b16(WIDTH), f32_leaf(LANES)]
    out_specs = [pl.BlockSpec((tm,) + s.shape[1:], lambda i, nd=len(s.shape): (i,) + (0,) * (nd - 1))
                 for s in out_shape]
    w_proj = wts["w_proj"]
    return pl.pallas_call(
        _proj_kernel,
        grid=(n // tm,),
        in_specs=[pl.BlockSpec((tm, D_MODEL), row),
                  pl.BlockSpec((1, D_MODEL), const),
                  _resident(w_proj.shape, const),
                  pl.BlockSpec((1, WIDTH), const),
                  pl.BlockSpec((1, WIDTH), const),
                  pl.BlockSpec((tm, LANES), tab),
                  pl.BlockSpec((tm, LANES), tab),
                  _resident((WIDTH, WIDTH), const)],
        out_specs=out_specs,
        out_shape=out_shape,
        compiler_params=pltpu.CompilerParams(dimension_semantics=("arbitrary",),
                                             vmem_limit_bytes=VMEM_LIMIT),
        name="proj",
    )(x2d, wts["g_mix"], w_proj, wts["g_qn"], wts["g_kn"], cs, sn, wts["grp"])


SUBLANES = 8
DSA_GROUP = 4
SB_GROUP = 4
COUNT_CHAINS = 4


HALF_MIN = -2 ** 15


def _key_to_float(k):
    return lax.bitcast_convert_type(k ^ ((k >> 31) & 0x7FFFFFFF), F32)


def _mask_heads(src_ref, dst_ref, group=2):
    tq = src_ref.shape[0]
    gw = group * HEAD_DIM
    head = lax.broadcasted_iota(jnp.int32, (tq, gw), 1) // HEAD_DIM
    for g in range(N_HEADS // group):
        cols = src_ref[:, g * gw:(g + 1) * gw]
        zero = jnp.zeros_like(cols)
        for h in range(group):
            dst_ref[g, h * tq:(h + 1) * tq, :] = jnp.where(head == h, cols, zero)


def _lanes(x, reps):
    return jnp.concatenate([x] * reps, axis=1) if reps > 1 else x


def _dsa_kernel(bound_ref, qa_ref, qi_ref, wi_ref, ki_ref, ka_ref, va_ref, o_ref,
                sc_ref, hi_ref, lo_ref, qam_ref, qim_ref, w_ref, shift_ref, l_ref, acc_ref,
                thr_ref, cut_ref, *, tq, tk, n_valid, q_off, topk, group):
    i = pl.program_id(1)
    q0 = q_off + i * tq
    qpos = q0 + lax.broadcasted_iota(jnp.int32, (1, tq), 1)
    chunk_end = lambda pos: ((pos >> CHUNK_SHIFT) + 1) << CHUNK_SHIFT
    lim = jnp.minimum(chunk_end(qpos), n_valid)
    lim16 = lim.astype(jnp.int16)
    n_adm = jnp.minimum(chunk_end(q0 + tq - 1), n_valid)
    nkb = (n_adm + tk - 1) // tk
    reps = tk // LANES
    gw = group * HEAD_DIM
    n_groups = N_HEADS // group
    key_iota = lax.broadcasted_iota(jnp.int32, (tk, 1), 0)

    _mask_heads(qi_ref, qim_ref, group)
    _mask_heads(qa_ref, qam_ref, group)
    w_t = wi_ref[...].T
    for g in range(n_groups):
        r0 = IDX_DIM + group * g
        row = jnp.concatenate([w_t[r0 + h:r0 + h + 1] for h in range(group)], axis=1)
        w_ref[g] = jnp.broadcast_to(row, (SUBLANES, group * tq))

    def score_block(j, carry):
        ks = pl.multiple_of(j * tk, tk)
        kib = _lanes(ki_ref[pl.ds(ks, tk), :], gw // LANES)
        score = jnp.zeros((tk, tq), F32)
        for g in range(n_groups):
            s = lax.dot_general(kib, qim_ref[g], _NT, preferred_element_type=F32)
            r = jnp.maximum(s, 0.0) * w_ref[g][:1]
            for h in range(group):
                score = score + r[:, h * tq:(h + 1) * tq]
        sc_ref[pl.ds(ks, tk), :] = jnp.where(ks + key_iota < lim, score, -jnp.inf)
        bits = lax.bitcast_convert_type(score, jnp.int32)
        hi = (bits >> 16).astype(jnp.int16)
        lo = bits.astype(jnp.int16)
        sign = jnp.where(hi < 0, jnp.int16(-1), jnp.int16(0))
        adm = (ks + key_iota).astype(jnp.int16) < lim16
        hi_ref[pl.ds(ks, tk), :] = jnp.where(adm, hi ^ (sign & jnp.int16(0x7FFF)),
                                             jnp.int16(HALF_MIN))
        lo_ref[pl.ds(ks, tk), :] = jnp.where(adm, lo ^ sign ^ jnp.int16(HALF_MIN),
                                             jnp.int16(HALF_MIN))
        return carry

    lax.fori_loop(0, nkb, score_block, 0)

    def count(src_ref, pred):
        dt = src_ref.dtype
        sub = SUBLANES * 4 // jnp.dtype(dt).itemsize
        one, zero = jnp.ones((), dt), jnp.zeros((), dt)

        def body(c, acc):
            ks = pl.multiple_of(c * tk, tk)
            blk = src_ref[pl.ds(ks, tk), :]
            parts = [jnp.zeros((sub, tq), dt) for _ in range(COUNT_CHAINS)]
            for r in range(tk // sub):
                hit = pred(blk[r * sub:(r + 1) * sub], ks + r * sub)
                parts[r % COUNT_CHAINS] = parts[r % COUNT_CHAINS] + jnp.where(hit, one, zero)
            part = functools.reduce(lambda a, b: a + b, parts).astype(F32)
            for half in range(sub // SUBLANES - 1):
                part = part[:SUBLANES] + part[SUBLANES:]
            return acc + part

        acc = lax.fori_loop(0, nkb, body, jnp.zeros((SUBLANES, tq), F32))
        return jnp.broadcast_to(jnp.sum(acc, axis=0, keepdims=True), (SUBLANES, tq))

    def halves(v):
        u = v.astype(jnp.int16)
        return jnp.concatenate([u, u], axis=0)

    def largest_half(src_ref, wanted):
        nonneg = count(src_ref, lambda blk, k0: blk >= jnp.int16(0)) >= wanted
        h0 = jnp.where(nonneg, jnp.zeros((SUBLANES, tq), jnp.int32),
                       jnp.full((SUBLANES, tq), HALF_MIN, jnp.int32))

        def bit_step(t, h):
            cand = h | (jnp.int32(1) << (14 - t))
            cand16 = halves(cand)
            return jnp.where(count(src_ref, lambda blk, k0: blk >= cand16) >= wanted, cand, h)

        return lax.fori_loop(0, 15, bit_step, h0)

    kf = float(topk)
    hi = largest_half(hi_ref, kf)
    hi16 = halves(hi)
    wanted_lo = kf - count(hi_ref, lambda blk, k0: blk > hi16)

    def keep_bucket(c, carry):
        ks = pl.multiple_of(c * tk, tk)
        in_bucket = hi_ref[pl.ds(ks, tk), :] == hi16[:1]
        lo_ref[pl.ds(ks, tk), :] = jnp.where(in_bucket, lo_ref[pl.ds(ks, tk), :],
                                             jnp.int16(HALF_MIN))
        return carry

    lax.fori_loop(0, nkb, keep_bucket, 0)
    lo = largest_half(lo_ref, wanted_lo)
    key = (hi << 16) | ((lo - HALF_MIN) & 0xFFFF)
    has_thr = key != INT_MIN
    thr = jnp.where(has_thr, _key_to_float(key), -jnp.inf)
    thr_ref[...] = thr
    need = kf - count(sc_ref, lambda blk, k0: blk > thr)
    c_eq = count(sc_ref, lambda blk, k0: blk == thr)
    cut_ref[...] = jnp.where(has_thr, jnp.int32(2 ** 30), jnp.int32(-1))
    split = has_thr & (c_eq > need)

    @pl.when(jnp.max(jnp.where(split, 1.0, 0.0)) > 0.0)
    def _():
        sub_iota = lax.broadcasted_iota(jnp.int32, (SUBLANES, 1), 0)

        def idx_step(t, cut):
            cand = cut | (jnp.int32(1) << (13 - t))
            below = count(sc_ref, lambda blk, k0: (blk == thr) & (k0 + sub_iota < cand))
            return jnp.where(below <= need - 1.0, cand, cut)

        cut = lax.fori_loop(0, 14, idx_step, jnp.zeros((SUBLANES, tq), jnp.int32))
        cut_ref[...] = jnp.where(has_thr, cut, -1)

    def masked_bias(ks):
        scb = sc_ref[pl.ds(ks, tk), :]
        thr_b = thr_ref[:1, :]
        sel = (scb > thr_b) | ((scb == thr_b) & (ks + key_iota <= cut_ref[:1, :]))
        bias = jnp.where(sel, 0.0, NEG_BIG).T
        return jnp.concatenate([bias] * group, axis=0)

    def logits(g, kab):
        return lax.dot_general(qam_ref[g], kab[:, g * gw:(g + 1) * gw], _NT,
                               preferred_element_type=F32)

    bound = bound_ref[0]
    shift_ref[...] = jnp.full(shift_ref.shape, bound, F32)

    @pl.when(bound > SOFTMAX_SAFE_BOUND)
    def _():
        l_ref[...] = jnp.full(l_ref.shape, NEG_BIG, F32)

        def max_block(j, carry):
            ks = pl.multiple_of(j * tk, tk)
            bias = masked_bias(ks)
            kab = ka_ref[pl.ds(ks, tk), :]
            for g in range(n_groups):
                s = logits(g, kab) + bias
                mx = l_ref[g]
                for u in range(reps):
                    mx = jnp.maximum(mx, s[:, u * LANES:(u + 1) * LANES])
                l_ref[g] = mx
            return carry

        lax.fori_loop(0, nkb, max_block, 0)
        for g in range(n_groups):
            shift_ref[g] = jnp.broadcast_to(jnp.max(l_ref[g], axis=1, keepdims=True),
                                            (group * tq, LANES))

    l_ref[...] = jnp.zeros(l_ref.shape, F32)
    acc_ref[...] = jnp.zeros(acc_ref.shape, F32)

    def attend_block(j, carry):
        ks = pl.multiple_of(j * tk, tk)
        bias = masked_bias(ks)
        kab = ka_ref[pl.ds(ks, tk), :]
        vab = va_ref[pl.ds(ks, tk), :]
        for g in range(n_groups):
            e = jnp.exp((logits(g, kab) - _lanes(shift_ref[g], reps)) + bias)
            part = l_ref[g]
            for u in range(reps):
                part = part + e[:, u * LANES:(u + 1) * LANES]
            l_ref[g] = part
            acc_ref[g] = acc_ref[g] + jnp.dot(
                e.astype(BF16), vab[:, g * gw:(g + 1) * gw], preferred_element_type=F32)
        return carry

    lax.fori_loop(0, nkb, attend_block, 0)

    head_o = lax.broadcasted_iota(jnp.int32, (tq, gw), 1) // HEAD_DIM
    for g in range(n_groups):
        out = acc_ref[g] / jnp.sum(l_ref[g], axis=1, keepdims=True)
        res = out[:tq]
        for h in range(1, group):
            res = jnp.where(head_o == h, out[h * tq:(h + 1) * tq], res)
        o_ref[:, g * gw:(g + 1) * gw] = res.astype(BF16)


def _dsa(bound, qa16, qi16, wi, ki16, ka16, va16, *, tq, tk, n_valid, q_off, group=DSA_GROUP):
    b, lq, _ = qa16.shape
    lk = ka16.shape[1]
    topk = min(TOPK_MAX, n_valid // 4)
    lq_pad = pl.cdiv(lq, LANES) * LANES
    if lq_pad != lq:
        pad = lambda a: jnp.pad(a, ((0, 0), (0, lq_pad - lq), (0, 0)))
        qa16, qi16, wi = pad(qa16), pad(qi16), pad(wi)
    tq = min(tq, lq_pad)
    assert tq % LANES == 0 and lq_pad % tq == 0 and lk % tk == 0
    assert lk < -HALF_MIN, "key indices are compared as int16"
    qblk = lambda bb, i: (bb, i, 0)
    kblk = lambda bb, i: (bb, 0, 0)
    kern = functools.partial(_dsa_kernel, tq=tq, tk=tk, n_valid=n_valid, q_off=q_off, topk=topk,
                             group=group)
    n_groups, gw = N_HEADS // group, group * HEAD_DIM
    stacked = lambda w, dt: pltpu.VMEM((n_groups, group * tq, w), dt)
    out = pl.pallas_call(
        kern,
        grid=(b, lq_pad // tq),
        in_specs=[pl.BlockSpec(memory_space=pltpu.SMEM),
                  pl.BlockSpec((None, tq, WIDTH), qblk),
                  pl.BlockSpec((None, tq, WIDTH), qblk),
                  pl.BlockSpec((None, tq, LANES), qblk),
                  _per_batch((None, lk, LANES), kblk, lq_pad // tq),
                  _per_batch((None, lk, WIDTH), kblk, lq_pad // tq),
                  _per_batch((None, lk, WIDTH), kblk, lq_pad // tq)],
        out_specs=pl.BlockSpec((None, tq, WIDTH), qblk),
        out_shape=jax.ShapeDtypeStruct((b, lq_pad, WIDTH), BF16),
        scratch_shapes=[pltpu.VMEM((lk, tq), F32),
                        pltpu.VMEM((lk, tq), jnp.int16),
                        pltpu.VMEM((lk, tq), jnp.int16),
                        stacked(gw, BF16),
                        stacked(gw, BF16),
                        pltpu.VMEM((n_groups, SUBLANES, group * tq), F32),
                        stacked(LANES, F32),
                        stacked(LANES, F32),
                        stacked(gw, F32),
                        pltpu.VMEM((SUBLANES, tq), F32),
                        pltpu.VMEM((SUBLANES, tq), jnp.int32)],
        compiler_params=pltpu.CompilerParams(dimension_semantics=("arbitrary", "arbitrary"),
                                             vmem_limit_bytes=VMEM_LIMIT),
        name="dsa",
    )(bound, qa16, qi16, wi, ki16, ka16, va16)
    return out[:, :lq]


def _sb_kernel(qb_ref, kb_ref, vb_ref, tri_ref, o_ref, qbm_ref, tail_ref, acc_ref,
               *, tq, tk, q_off, group):
    i = pl.program_id(1)
    q0 = q_off + i * tq
    gw = group * HEAD_DIM
    n_groups = N_HEADS // group
    qpos = q0 + lax.broadcasted_iota(jnp.int32, (tq, 1), 0)
    qpos = jnp.concatenate([qpos] * group, axis=0)
    lane_kpos = lax.broadcasted_iota(jnp.int32, (1, tk), 1)
    tri = tri_ref[...]

    _mask_heads(qb_ref, qbm_ref, group)
    tail_ref[...] = jnp.zeros(tail_ref.shape, F32)
    acc_ref[...] = jnp.zeros(acc_ref.shape, F32)

    def block(state):
        j, _ = state
        ks = pl.multiple_of(j * tk, tk)
        causal = (ks + lane_kpos) < qpos
        kbb = kb_ref[pl.ds(ks, tk), :]
        vbb = vb_ref[pl.ds(ks, tk), :]
        tail_min = jnp.full((group * tq, 1), jnp.inf, F32)
        for p in range(n_groups):
            cols = slice(p * gw, (p + 1) * gw)
            z = lax.dot_general(qbm_ref[p], kbb[:, cols], _NT, preferred_element_type=F32)
            lp = jnp.log(1.0 + jnp.exp(-jnp.abs(z)))
            sp = jnp.where(causal, jnp.maximum(z, 0.0) + lp, 0.0)
            hi = sp.astype(BF16)
            lo = (sp - hi.astype(F32)).astype(BF16)
            inner = (jnp.dot(hi, tri, preferred_element_type=F32)
                     + jnp.dot(lo, tri, preferred_element_type=F32))
            tail_old = tail_ref[p]
            log_a = (jnp.minimum(z, 0.0) - lp) - inner - tail_old[:, :1]
            a = jnp.where(causal, jnp.exp(log_a), 0.0)
            acc_ref[p] = acc_ref[p] + jnp.dot(a.astype(BF16), vbb[:, cols],
                                              preferred_element_type=F32)
            tail_new = tail_old + jnp.sum(sp, axis=1, keepdims=True)
            tail_ref[p] = tail_new
            tail_min = jnp.minimum(tail_min, tail_new[:, :1])
        return j - 1, jnp.min(tail_min)

    def more(state):
        j, tail_min = state
        return (j >= 0) & (tail_min <= SB_EXIT_TAIL)

    lax.while_loop(more, block, ((q0 + tq - 1) // tk, jnp.float32(0.0)))

    head_o = lax.broadcasted_iota(jnp.int32, (tq, gw), 1) // HEAD_DIM
    for p in range(n_groups):
        acc = acc_ref[p]
        res = acc[:tq]
        for h in range(1, group):
            res = jnp.where(head_o == h, acc[h * tq:(h + 1) * tq], res)
        o_ref[:, p * gw:(p + 1) * gw] = res.astype(BF16)


def _sb(qb16, kb16, vb16, tri, *, tq, tk, q_off, group=SB_GROUP):
    b, lq, _ = qb16.shape
    lk = kb16.shape[1]
    qblk = lambda bb, i: (bb, i, 0)
    kblk = lambda bb, i: (bb, 0, 0)
    kern = functools.partial(_sb_kernel, tq=tq, tk=tk, q_off=q_off, group=group)
    n_groups, gw = N_HEADS // group, group * HEAD_DIM
    return pl.pallas_call(
        kern,
        grid=(b, lq // tq),
        in_specs=[pl.BlockSpec((None, tq, WIDTH), qblk),
                  pl.BlockSpec((None, lk, WIDTH), kblk),
                  pl.BlockSpec((None, lk, WIDTH), kblk),
                  _resident((tk, tk), lambda bb, i: (0, 0))],
        out_specs=pl.BlockSpec((None, tq, WIDTH), qblk),
        out_shape=jax.ShapeDtypeStruct((b, lq, WIDTH), BF16),
        scratch_shapes=[pltpu.VMEM((n_groups, group * tq, gw), BF16),
                        pltpu.VMEM((n_groups, group * tq, LANES), F32),
                        pltpu.VMEM((n_groups, group * tq, gw), F32)],
        compiler_params=pltpu.CompilerParams(dimension_semantics=("arbitrary", "arbitrary"),
                                             vmem_limit_bytes=VMEM_LIMIT),
        name="sb",
    )(qb16, kb16, vb16, tri)


FF_CHUNK = 1024


def _rms_rows(x, g):
    return x * lax.rsqrt(jnp.mean(x * x, axis=-1, keepdims=True) + EPS) * g


def _merge_kernel(x_ref, oa_ref, ob_ref, gmix_ref, gffn_ref, wg_ref, wba_ref, wbb_ref,
                  wout_ref, wup_ref, wdown_ref, y_ref):
    x = x_ref[...]
    hx = _rms_rows(x, gmix_ref[...]).astype(BF16)
    gate_a = jnp.dot(hx, wg_ref[:, :D_MODEL], preferred_element_type=F32)
    gate_b = jnp.dot(hx, wg_ref[:, D_MODEL:], preferred_element_type=F32)
    pa = jnp.dot(oa_ref[...], wba_ref[...], preferred_element_type=F32)
    pb = jnp.dot(ob_ref[...], wbb_ref[...], preferred_element_type=F32)
    m = jax.nn.sigmoid(gate_a) * pa + jax.nn.sigmoid(gate_b) * pb
    h = x + jnp.dot(m.astype(BF16), wout_ref[...], preferred_element_type=F32)
    hn = _rms_rows(h, gffn_ref[...]).astype(BF16)
    y = h
    for c in range(D_FF // FF_CHUNK):
        u = jnp.dot(hn, wup_ref[:, c * FF_CHUNK:(c + 1) * FF_CHUNK], preferred_element_type=F32)
        r = jnp.square(jnp.maximum(u, 0.0)).astype(BF16)
        y = y + jnp.dot(r, wdown_ref[c * FF_CHUNK:(c + 1) * FF_CHUNK, :],
                        preferred_element_type=F32)
    y_ref[...] = y


def _merge_ffn(x2d, oa16, ob16, wts, tm):
    n = x2d.shape[0]
    row = lambda i: (i, 0)
    const = lambda i: (0, 0)
    weights = [wts["w_gate"], wts["w_branch_a"], wts["w_branch_b"], wts["w_out"],
               wts["w_up"], wts["w_down"]]
    return pl.pallas_call(
        _merge_kernel,
        grid=(n // tm,),
        in_specs=[pl.BlockSpec((tm, D_MODEL), row),
                  pl.BlockSpec((tm, WIDTH), row),
                  pl.BlockSpec((tm, WIDTH), row),
                  pl.BlockSpec((1, D_MODEL), const),
                  pl.BlockSpec((1, D_MODEL), const)]
                 + [_resident(w.shape, const) for w in weights],
        out_specs=pl.BlockSpec((tm, D_MODEL), row),
        out_shape=jax.ShapeDtypeStruct((n, D_MODEL), F32),
        compiler_params=pltpu.CompilerParams(dimension_semantics=("arbitrary",),
                                             vmem_limit_bytes=VMEM_LIMIT),
        name="merge_ffn",
    )(x2d, oa16, ob16, wts["g_mix"], wts["g_ffn"], *weights)


def _rope_tables(pos, rows):
    half = HEAD_DIM // 2
    inv_freq = jnp.power(ROPE_THETA, -jnp.arange(half, dtype=F32) / half)
    ang = pos.astype(F32)[:, None] * inv_freq[None, :]
    cos, sin = jnp.cos(ang), jnp.sin(ang)
    cs = jnp.concatenate([cos, cos, cos, cos], axis=1)
    sn = jnp.concatenate([-sin, sin, -sin, sin], axis=1)
    reps = rows // pos.shape[0]
    if reps > 1:
        cs, sn = jnp.tile(cs, (reps, 1)), jnp.tile(sn, (reps, 1))
    return cs, sn


def _prep_weights(g_mix, w_in, g_qn, g_kn, w_branch_a, w_branch_b, w_out, g_ffn, w_up, w_down):
    offs = [int(v) for v in np.cumsum(SPLIT_SIZES)[:-1]]
    w_qa, w_ka, w_va, w_qi, w_ki, w_wi, w_qb, w_kb, w_vb, w_ga, w_gb = jnp.split(w_in, offs, axis=1)
    pad = jnp.zeros((D_MODEL, LANES - IDX_DIM - N_HEADS), w_in.dtype)
    w_proj = jnp.concatenate([w_qa, w_ka, w_va, w_qi, w_qb, w_kb, w_vb, w_ki, w_wi, pad], axis=1)
    head = np.arange(WIDTH) // HEAD_DIM
    return {
        "w_proj": w_proj.astype(BF16),
        "w_gate": jnp.concatenate([w_ga, w_gb], axis=1).astype(BF16),
        "grp": jnp.asarray(head[:, None] == head[None, :], BF16),
        "g_mix": g_mix.reshape(1, D_MODEL),
        "g_ffn": g_ffn.reshape(1, D_MODEL),
        "g_qn": jnp.tile(g_qn, N_HEADS).reshape(1, WIDTH),
        "g_kn": jnp.tile(g_kn, N_HEADS).reshape(1, WIDTH),
        "logit_bound": (HEAD_DIM * QK_SCALE * jnp.max(jnp.abs(g_qn))
                        * jnp.max(jnp.abs(g_kn))).reshape(1).astype(F32),
        "w_branch_a": w_branch_a.astype(BF16),
        "w_branch_b": w_branch_b.astype(BF16),
        "w_out": w_out.astype(BF16),
        "w_up": w_up.astype(BF16),
        "w_down": w_down.astype(BF16),
    }


def _tri(tk):
    r = np.arange(tk)
    return jnp.asarray(r[:, None] > r[None, :], BF16)


def _pad_keys(a, lk):
    return jnp.pad(a, ((0, 0), (0, lk - a.shape[1]), (0, 0)))


def _layer(x, pos, wts, caches, *, tq_dsa, tk_dsa, tq_sb, tk_sb):
    b, l, _ = x.shape
    n = b * l
    tm = min(512, n)
    x2d = x.reshape(n, D_MODEL)
    rows = l if l >= tm else tm
    (ka, va, ki, kb, vb, qa16, ka16, va16, qi16, ki16, qb16, kb16, vb16, wi) = _project(
        x2d, _rope_tables(pos, rows), wts, tm)
    per_seq = lambda a: a.reshape(b, l, a.shape[-1])
    qa16, qi16, qb16, wi = per_seq(qa16), per_seq(qi16), per_seq(qb16), per_seq(wi)
    ka16, va16, ki16, kb16, vb16 = (per_seq(a) for a in (ka16, va16, ki16, kb16, vb16))
    past = 0 if caches is None else caches[0].shape[1]
    n_valid = past + l
    lk_dsa = pl.cdiv(n_valid, tk_dsa) * tk_dsa
    lk_sb = pl.cdiv(n_valid, tk_sb) * tk_sb

    def keys(new16, cache, lk, dup=1):
        if cache is None:
            return _pad_keys(new16, lk)
        c16 = cache.reshape(b, past, -1).astype(BF16)
        if dup > 1:
            c16 = jnp.concatenate([c16] * dup, axis=2)
        return lax.dynamic_update_slice(_pad_keys(c16, lk), new16, (0, past, 0))

    c_ka, c_va, c_ki, c_kb, c_vb = caches if caches is not None else (None,) * 5
    oa16 = _dsa(wts["logit_bound"], qa16, qi16, wi, keys(ki16, c_ki, lk_dsa, dup=2),
                keys(ka16, c_ka, lk_dsa), keys(va16, c_va, lk_dsa),
                tq=tq_dsa, tk=tk_dsa, n_valid=n_valid, q_off=past)
    ob16 = _sb(qb16, keys(kb16, c_kb, lk_sb), keys(vb16, c_vb, lk_sb), _tri(tk_sb),
               tq=tq_sb, tk=tk_sb, q_off=past)
    y = _merge_ffn(x2d, oa16.reshape(n, WIDTH), ob16.reshape(n, WIDTH), wts, tm)
    heads = lambda a: a.reshape(b, l, N_HEADS, HEAD_DIM)
    return (y.reshape(b, l, D_MODEL), heads(ka), heads(va), ki.reshape(b, l, IDX_DIM),
            heads(kb), heads(vb))


def kernel(x_prompt, x_sample, cache_k_a, cache_v_a, cache_k_idx, cache_k_sb, cache_v_sb,
           g_mix, w_in, g_qn, g_kn, w_branch_a, w_branch_b, w_out, g_ffn, w_up, w_down):
    wts = _prep_weights(g_mix, w_in, g_qn, g_kn, w_branch_a, w_branch_b, w_out, g_ffn, w_up, w_down)
    s = x_prompt.shape[1]
    t = x_sample.shape[1]
    past = cache_k_a.shape[1]
    prompt = _layer(x_prompt, jnp.arange(s, dtype=jnp.int32), wts, None,
                    tq_dsa=min(256, s), tk_dsa=min(1024, s), tq_sb=min(256, s), tk_sb=min(256, s))
    sample = _layer(x_sample, past + jnp.arange(t, dtype=jnp.int32), wts,
                    (cache_k_a, cache_v_a, cache_k_idx, cache_k_sb, cache_v_sb),
                    tq_dsa=LANES, tk_dsa=512, tq_sb=t, tk_sb=128)
    return (prompt[0], sample[0]) + prompt[1:] + sample[1:]
```

```python
import functools

import numpy as np
import jax
import jax.numpy as jnp
from jax import lax
from jax.experimental import pallas as pl
from jax.experimental.pallas import tpu as pltpu

D_MODEL = 1024
HEAD_DIM = 64
N_HEADS = 8
WIDTH = N_HEADS * HEAD_DIM
IDX_DIM = 64
CHUNK = 64
CHUNK_SHIFT = CHUNK.bit_length() - 1
assert 1 << CHUNK_SHIFT == CHUNK
TOPK_MAX = 256
D_FF = 4 * D_MODEL
ROPE_THETA = 10000.0
EPS = 1e-6
SPLIT_SIZES = (WIDTH, WIDTH, WIDTH, N_HEADS * IDX_DIM, IDX_DIM, N_HEADS,
               WIDTH, WIDTH, WIDTH, D_MODEL, D_MODEL)

LANES = 128
PAIR = 2 * HEAD_DIM
N_PAIRS = N_HEADS // 2
QK_SCALE = HEAD_DIM ** -0.5
W_IDX_SCALE = (N_HEADS ** -0.5) * (IDX_DIM ** -0.5)
INT_MIN = -2 ** 31
NEG_BIG = -1e30
SOFTMAX_SAFE_BOUND = 40.0
SB_EXIT_TAIL = 105.0
VMEM_LIMIT = 56 * 1024 * 1024

F32 = jnp.float32
BF16 = jnp.bfloat16
_NT = (((1,), (1,)), ((), ()))


def _resident(shape, index_map):
    return pl.BlockSpec(shape, index_map, pipeline_mode=pl.Buffered(1))


def _per_batch(shape, index_map, steps_per_batch):
    if steps_per_batch == 1:
        return pl.BlockSpec(shape, index_map)
    return _resident(shape, index_map)


def _rope(x, cs, sn):
    w = x.shape[1]
    lane = lax.broadcasted_iota(jnp.int32, x.shape, 1)
    first_half = (lane & (HEAD_DIM // 2)) == 0
    partner = jnp.where(first_half, pltpu.roll(x, w - HEAD_DIM // 2, 1),
                        pltpu.roll(x, HEAD_DIM // 2, 1))
    reps = w // LANES
    if reps > 1:
        cs = jnp.concatenate([cs] * reps, axis=1)
        sn = jnp.concatenate([sn] * reps, axis=1)
    return x * cs + partner * sn


def _head_rmsnorm(z, grp, g):
    zz = z * z
    hi = zz.astype(BF16)
    lo = (zz - hi.astype(F32)).astype(BF16)
    ss = (jnp.dot(hi, grp, preferred_element_type=F32)
          + jnp.dot(lo, grp, preferred_element_type=F32))
    return z * lax.rsqrt(ss * (1.0 / HEAD_DIM) + EPS) * g


def _proj_kernel(x_ref, gmix_ref, w_ref, gqn_ref, gkn_ref, cs_ref, sn_ref, grp_ref,
                 ka_ref, va_ref, ki_ref, kb_ref, vb_ref,
                 qa16_ref, ka16_ref, va16_ref, qi16_ref, ki16_ref,
                 qb16_ref, kb16_ref, vb16_ref, wi_ref):
    x = x_ref[...]
    h = (x * lax.rsqrt(jnp.mean(x * x, axis=-1, keepdims=True) + EPS)
         * gmix_ref[...]).astype(BF16)
    cs = cs_ref[...]
    sn = sn_ref[...]
    grp = grp_ref[...]

    def seg(k, width=WIDTH):
        return jnp.dot(h, w_ref[:, k * WIDTH:k * WIDTH + width],
                       preferred_element_type=F32)

    split_heads = lambda z: pltpu.einshape("m(hd)->mhd", z, h=N_HEADS)

    qa = _rope(_head_rmsnorm(seg(0), grp, gqn_ref[...]), cs, sn)
    qa16_ref[...] = (qa * QK_SCALE).astype(BF16)
    ka = _rope(_head_rmsnorm(seg(1), grp, gkn_ref[...]), cs, sn)
    ka_ref[...] = split_heads(ka)
    ka16_ref[...] = ka.astype(BF16)
    va = seg(2)
    va_ref[...] = split_heads(va)
    va16_ref[...] = va.astype(BF16)
    qi16_ref[...] = _rope(seg(3), cs, sn).astype(BF16)
    qb16_ref[...] = (seg(4) * QK_SCALE).astype(BF16)
    kb = seg(5)
    kb_ref[...] = split_heads(kb)
    kb16_ref[...] = kb.astype(BF16)
    vb = seg(6)
    vb_ref[...] = split_heads(vb)
    vb16_ref[...] = vb.astype(BF16)
    misc = seg(7, LANES)
    ki = _rope(misc, cs, sn)[:, :IDX_DIM]
    ki_ref[...] = ki
    ki16 = ki.astype(BF16)
    ki16_ref[...] = jnp.concatenate([ki16, ki16], axis=1)
    wi_ref[...] = misc * W_IDX_SCALE


def _project(x2d, tabs, wts, tm):
    n = x2d.shape[0]
    cs, sn = tabs
    nt = cs.shape[0] // tm
    row = lambda i: (i, 0)
    const = lambda i: (0, 0)
    tab = lambda i: (i % nt, 0)
    f32_leaf = lambda w: jax.ShapeDtypeStruct((n, w), F32)
    heads_leaf = jax.ShapeDtypeStruct((n, N_HEADS, HEAD_DIM), F32)
    b16 = lambda w: jax.ShapeDtypeStruct((n, w), BF16)
    out_shape = [heads_leaf, heads_leaf, f32_leaf(IDX_DIM), heads_leaf, heads_leaf,
                 b16(WIDTH), b16(WIDTH), b16(WIDTH), b16(WIDTH), b16(LANES),
                 b16(WIDTH), b16(WIDTH), b16(WIDTH), f32_leaf(LANES)]
    out_specs = [pl.BlockSpec((tm,) + s.shape[1:], lambda i, nd=len(s.shape): (i,) + (0,) * (nd - 1))
                 for s in out_shape]
    w_proj = wts["w_proj"]
    return pl.pallas_call(
        _proj_kernel,
        grid=(n // tm,),
        in_specs=[pl.BlockSpec((tm, D_MODEL), row),
                  pl.BlockSpec((1, D_MODEL), const),
                  _resident(w_proj.shape, const),
                  pl.BlockSpec((1, WIDTH), const),
                  pl.BlockSpec((1, WIDTH), const),
                  pl.BlockSpec((tm, LANES), tab),
                  pl.BlockSpec((tm, LANES), tab),
                  _resident((WIDTH, WIDTH), const)],
        out_specs=out_specs,
        out_shape=out_shape,
        compiler_params=pltpu.CompilerParams(dimension_semantics=("arbitrary",),
                                             vmem_limit_bytes=VMEM_LIMIT),
        name="proj",
    )(x2d, wts["g_mix"], w_proj, wts["g_qn"], wts["g_kn"], cs, sn, wts["grp"])


SUBLANES = 8
DSA_GROUP = 4
SB_GROUP = 4
COUNT_CHAINS = 4


HALF_MIN = -2 ** 15


def _key_to_float(k):
    return lax.bitcast_convert_type(k ^ ((k >> 31) & 0x7FFFFFFF), F32)


def _mask_heads(src_ref, dst_ref, group=2):
    tq = src_ref.shape[0]
    gw = group * HEAD_DIM
    head = lax.broadcasted_iota(jnp.int32, (tq, gw), 1) // HEAD_DIM
    for g in range(N_HEADS // group):
        cols = src_ref[:, g * gw:(g + 1) * gw]
        zero = jnp.zeros_like(cols)
        for h in range(group):
            dst_ref[g, h * tq:(h + 1) * tq, :] = jnp.where(head == h, cols, zero)


def _lanes(x, reps):
    return jnp.concatenate([x] * reps, axis=1) if reps > 1 else x


def _dsa_kernel(bound_ref, qa_ref, qi_ref, wi_ref, ki_ref, ka_ref, va_ref, o_ref,
                sc_ref, hi_ref, lo_ref, qam_ref, qim_ref, w_ref, shift_ref, l_ref, acc_ref,
                thr_ref, cut_ref, *, tq, tk, n_valid, q_off, topk, group):
    i = pl.program_id(1)
    q0 = q_off + i * tq
    qpos = q0 + lax.broadcasted_iota(jnp.int32, (1, tq), 1)
    chunk_end = lambda pos: ((pos >> CHUNK_SHIFT) + 1) << CHUNK_SHIFT
    lim = jnp.minimum(chunk_end(qpos), n_valid)
    lim16 = lim.astype(jnp.int16)
    n_adm = jnp.minimum(chunk_end(q0 + tq - 1), n_valid)
    nkb = (n_adm + tk - 1) // tk
    reps = tk // LANES
    gw = group * HEAD_DIM
    n_groups = N_HEADS // group
    key_iota = lax.broadcasted_iota(jnp.int32, (tk, 1), 0)

    _mask_heads(qi_ref, qim_ref, group)
    _mask_heads(qa_ref, qam_ref, group)
    w_t = wi_ref[...].T
    for g in range(n_groups):
        r0 = IDX_DIM + group * g
        row = jnp.concatenate([w_t[r0 + h:r0 + h + 1] for h in range(group)], axis=1)
        w_ref[g] = jnp.broadcast_to(row, (SUBLANES, group * tq))

    def score_block(j, carry):
        ks = pl.multiple_of(j * tk, tk)
        kib = _lanes(ki_ref[pl.ds(ks, tk), :], gw // LANES)
        score = jnp.zeros((tk, tq), F32)
        for g in range(n_groups):
            s = lax.dot_general(kib, qim_ref[g], _NT, preferred_element_type=F32)
            r = jnp.maximum(s, 0.0) * w_ref[g][:1]
            for h in range(group):
                score = score + r[:, h * tq:(h + 1) * tq]
        sc_ref[pl.ds(ks, tk), :] = jnp.where(ks + key_iota < lim, score, -jnp.inf)
        bits = lax.bitcast_convert_type(score, jnp.int32)
        hi = (bits >> 16).astype(jnp.int16)
        lo = bits.astype(jnp.int16)
        sign = jnp.where(hi < 0, jnp.int16(-1), jnp.int16(0))
        adm = (ks + key_iota).astype(jnp.int16) < lim16
        hi_ref[pl.ds(ks, tk), :] = jnp.where(adm, hi ^ (sign & jnp.int16(0x7FFF)),
                                             jnp.int16(HALF_MIN))
        lo_ref[pl.ds(ks, tk), :] = jnp.where(adm, lo ^ sign ^ jnp.int16(HALF_MIN),
                                             jnp.int16(HALF_MIN))
        return carry

    lax.fori_loop(0, nkb, score_block, 0)

    def count(src_ref, pred):
        dt = src_ref.dtype
        sub = SUBLANES * 4 // jnp.dtype(dt).itemsize
        one, zero = jnp.ones((), dt), jnp.zeros((), dt)

        def body(c, acc):
            ks = pl.multiple_of(c * tk, tk)
            blk = src_ref[pl.ds(ks, tk), :]
            parts = [jnp.zeros((sub, tq), dt) for _ in range(COUNT_CHAINS)]
            for r in range(tk // sub):
                hit = pred(blk[r * sub:(r + 1) * sub], ks + r * sub)
                parts[r % COUNT_CHAINS] = parts[r % COUNT_CHAINS] + jnp.where(hit, one, zero)
            part = functools.reduce(lambda a, b: a + b, parts).astype(F32)
            for half in range(sub // SUBLANES - 1):
                part = part[:SUBLANES] + part[SUBLANES:]
            return acc + part

        acc = lax.fori_loop(0, nkb, body, jnp.zeros((SUBLANES, tq), F32))
        return jnp.broadcast_to(jnp.sum(acc, axis=0, keepdims=True), (SUBLANES, tq))

    def halves(v):
        u = v.astype(jnp.int16)
        return jnp.concatenate([u, u], axis=0)

    def largest_half(src_ref, wanted):
        nonneg = count(src_ref, lambda blk, k0: blk >= jnp.int16(0)) >= wanted
        h0 = jnp.where(nonneg, jnp.zeros((SUBLANES, tq), jnp.int32),
                       jnp.full((SUBLANES, tq), HALF_MIN, jnp.int32))

        def bit_step(t, h):
            cand = h | (jnp.int32(1) << (14 - t))
            cand16 = halves(cand)
            return jnp.where(count(src_ref, lambda blk, k0: blk >= cand16) >= wanted, cand, h)

        return lax.fori_loop(0, 15, bit_step, h0)

    kf = float(topk)
    hi = largest_half(hi_ref, kf)
    hi16 = halves(hi)
    wanted_lo = kf - count(hi_ref, lambda blk, k0: blk > hi16)

    def keep_bucket(c, carry):
        ks = pl.multiple_of(c * tk, tk)
        in_bucket = hi_ref[pl.ds(ks, tk), :] == hi16[:1]
        lo_ref[pl.ds(ks, tk), :] = jnp.where(in_bucket, lo_ref[pl.ds(ks, tk), :],
                                             jnp.int16(HALF_MIN))
        return carry

    lax.fori_loop(0, nkb, keep_bucket, 0)
    lo = largest_half(lo_ref, wanted_lo)
    key = (hi << 16) | ((lo - HALF_MIN) & 0xFFFF)
    has_thr = key != INT_MIN
    thr = jnp.where(has_thr, _key_to_float(key), -jnp.inf)
    thr_ref[...] = thr
    need = kf - count(sc_ref, lambda blk, k0: blk > thr)
    c_eq = count(sc_ref, lambda blk, k0: blk == thr)
    cut_ref[...] = jnp.where(has_thr, jnp.int32(2 ** 30), jnp.int32(-1))
    split = has_thr & (c_eq > need)

    @pl.when(jnp.max(jnp.where(split, 1.0, 0.0)) > 0.0)
    def _():
        sub_iota = lax.broadcasted_iota(jnp.int32, (SUBLANES, 1), 0)

        def idx_step(t, cut):
            cand = cut | (jnp.int32(1) << (13 - t))
            below = count(sc_ref, lambda blk, k0: (blk == thr) & (k0 + sub_iota < cand))
            return jnp.where(below <= need - 1.0, cand, cut)

        cut = lax.fori_loop(0, 14, idx_step, jnp.zeros((SUBLANES, tq), jnp.int32))
        cut_ref[...] = jnp.where(has_thr, cut, -1)

    def masked_bias(ks):
        scb = sc_ref[pl.ds(ks, tk), :]
        thr_b = thr_ref[:1, :]
        sel = (scb > thr_b) | ((scb == thr_b) & (ks + key_iota <= cut_ref[:1, :]))
        bias = jnp.where(sel, 0.0, NEG_BIG).T
        return jnp.concatenate([bias] * group, axis=0)

    def logits(g, kab):
        return lax.dot_general(qam_ref[g], kab[:, g * gw:(g + 1) * gw], _NT,
                               preferred_element_type=F32)

    bound = bound_ref[0]
    shift_ref[...] = jnp.full(shift_ref.shape, bound, F32)

    @pl.when(bound > SOFTMAX_SAFE_BOUND)
    def _():
        l_ref[...] = jnp.full(l_ref.shape, NEG_BIG, F32)

        def max_block(j, carry):
            ks = pl.multiple_of(j * tk, tk)
            bias = masked_bias(ks)
            kab = ka_ref[pl.ds(ks, tk), :]
            for g in range(n_groups):
                s = logits(g, kab) + bias
                mx = l_ref[g]
                for u in range(reps):
                    mx = jnp.maximum(mx, s[:, u * LANES:(u + 1) * LANES])
                l_ref[g] = mx
            return carry

        lax.fori_loop(0, nkb, max_block, 0)
        for g in range(n_groups):
            shift_ref[g] = jnp.broadcast_to(jnp.max(l_ref[g], axis=1, keepdims=True),
                                            (group * tq, LANES))

    l_ref[...] = jnp.zeros(l_ref.shape, F32)
    acc_ref[...] = jnp.zeros(acc_ref.shape, F32)

    def attend_block(j, carry):
        ks = pl.multiple_of(j * tk, tk)
        bias = masked_bias(ks)
        kab = ka_ref[pl.ds(ks, tk), :]
        vab = va_ref[pl.ds(ks, tk), :]
        for g in range(n_groups):
            e = jnp.exp((logits(g, kab) - _lanes(shift_ref[g], reps)) + bias)
            part = l_ref[g]
            for u in range(reps):
                part = part + e[:, u * LANES:(u + 1) * LANES]
            l_ref[g] = part
            acc_ref[g] = acc_ref[g] + jnp.dot(
                e.astype(BF16), vab[:, g * gw:(g + 1) * gw], preferred_element_type=F32)
        return carry

    lax.fori_loop(0, nkb, attend_block, 0)

    head_o = lax.broadcasted_iota(jnp.int32, (tq, gw), 1) // HEAD_DIM
    for g in range(n_groups):
        out = acc_ref[g] / jnp.sum(l_ref[g], axis=1, keepdims=True)
        res = out[:tq]
        for h in range(1, group):
            res = jnp.where(head_o == h, out[h * tq:(h + 1) * tq], res)
        o_ref[:, g * gw:(g + 1) * gw] = res.astype(BF16)


def _dsa(bound, qa16, qi16, wi, ki16, ka16, va16, *, tq, tk, n_valid, q_off, group=DSA_GROUP):
    b, lq, _ = qa16.shape
    lk = ka16.shape[1]
    topk = min(TOPK_MAX, n_valid // 4)
    lq_pad = pl.cdiv(lq, LANES) * LANES
    if lq_pad != lq:
        pad = lambda a: jnp.pad(a, ((0, 0), (0, lq_pad - lq), (0, 0)))
        qa16, qi16, wi = pad(qa16), pad(qi16), pad(wi)
    tq = min(tq, lq_pad)
    assert tq % LANES == 0 and lq_pad % tq == 0 and lk % tk == 0
    assert lk < -HALF_MIN, "key indices are compared as int16"
    qblk = lambda bb, i: (bb, i, 0)
    kblk = lambda bb, i: (bb, 0, 0)
    kern = functools.partial(_dsa_kernel, tq=tq, tk=tk, n_valid=n_valid, q_off=q_off, topk=topk,
                             group=group)
    n_groups, gw = N_HEADS // group, group * HEAD_DIM
    stacked = lambda w, dt: pltpu.VMEM((n_groups, group * tq, w), dt)
    out = pl.pallas_call(
        kern,
        grid=(b, lq_pad // tq),
        in_specs=[pl.BlockSpec(memory_space=pltpu.SMEM),
                  pl.BlockSpec((None, tq, WIDTH), qblk),
                  pl.BlockSpec((None, tq, WIDTH), qblk),
                  pl.BlockSpec((None, tq, LANES), qblk),
                  _per_batch((None, lk, LANES), kblk, lq_pad // tq),
                  _per_batch((None, lk, WIDTH), kblk, lq_pad // tq),
                  _per_batch((None, lk, WIDTH), kblk, lq_pad // tq)],
        out_specs=pl.BlockSpec((None, tq, WIDTH), qblk),
        out_shape=jax.ShapeDtypeStruct((b, lq_pad, WIDTH), BF16),
        scratch_shapes=[pltpu.VMEM((lk, tq), F32),
                        pltpu.VMEM((lk, tq), jnp.int16),
                        pltpu.VMEM((lk, tq), jnp.int16),
                        stacked(gw, BF16),
                        stacked(gw, BF16),
                        pltpu.VMEM((n_groups, SUBLANES, group * tq), F32),
                        stacked(LANES, F32),
                        stacked(LANES, F32),
                        stacked(gw, F32),
                        pltpu.VMEM((SUBLANES, tq), F32),
                        pltpu.VMEM((SUBLANES, tq), jnp.int32)],
        compiler_params=pltpu.CompilerParams(dimension_semantics=("arbitrary", "arbitrary"),
                                             vmem_limit_bytes=VMEM_LIMIT),
        name="dsa",
    )(bound, qa16, qi16, wi, ki16, ka16, va16)
    return out[:, :lq]


def _sb_kernel(qb_ref, kb_ref, vb_ref, tri_ref, o_ref, qbm_ref, tail_ref, acc_ref,
               *, tq, tk, q_off, group):
    i = pl.program_id(1)
    q0 = q_off + i * tq
    gw = group * HEAD_DIM
    n_groups = N_HEADS // group
    qpos = q0 + lax.broadcasted_iota(jnp.int32, (tq, 1), 0)
    qpos = jnp.concatenate([qpos] * group, axis=0)
    lane_kpos = lax.broadcasted_iota(jnp.int32, (1, tk), 1)
    tri = tri_ref[...]

    _mask_heads(qb_ref, qbm_ref, group)
    tail_ref[...] = jnp.zeros(tail_ref.shape, F32)
    acc_ref[...] = jnp.zeros(acc_ref.shape, F32)

    def block(state):
        j, _ = state
        ks = pl.multiple_of(j * tk, tk)
        causal = (ks + lane_kpos) < qpos
        kbb = kb_ref[pl.ds(ks, tk), :]
        vbb = vb_ref[pl.ds(ks, tk), :]
        tail_min = jnp.full((group * tq, 1), jnp.inf, F32)
        for p in range(n_groups):
            cols = slice(p * gw, (p + 1) * gw)
            z = lax.dot_general(qbm_ref[p], kbb[:, cols], _NT, preferred_element_type=F32)
            lp = jnp.log(1.0 + jnp.exp(-jnp.abs(z)))
            sp = jnp.where(causal, jnp.maximum(z, 0.0) + lp, 0.0)
            hi = sp.astype(BF16)
            lo = (sp - hi.astype(F32)).astype(BF16)
            inner = (jnp.dot(hi, tri, preferred_element_type=F32)
                     + jnp.dot(lo, tri, preferred_element_type=F32))
            tail_old = tail_ref[p]
            log_a = (jnp.minimum(z, 0.0) - lp) - inner - tail_old[:, :1]
            a = jnp.where(causal, jnp.exp(log_a), 0.0)
            acc_ref[p] = acc_ref[p] + jnp.dot(a.astype(BF16), vbb[:, cols],
                                              preferred_element_type=F32)
            tail_new = tail_old + jnp.sum(sp, axis=1, keepdims=True)
            tail_ref[p] = tail_new
            tail_min = jnp.minimum(tail_min, tail_new[:, :1])
        return j - 1, jnp.min(tail_min)

    def more(state):
        j, tail_min = state
        return (j >= 0) & (tail_min <= SB_EXIT_TAIL)

    lax.while_loop(more, block, ((q0 + tq - 1) // tk, jnp.float32(0.0)))

    head_o = lax.broadcasted_iota(jnp.int32, (tq, gw), 1) // HEAD_DIM
    for p in range(n_groups):
        acc = acc_ref[p]
        res = acc[:tq]
        for h in range(1, group):
            res = jnp.where(head_o == h, acc[h * tq:(h + 1) * tq], res)
        o_ref[:, p * gw:(p + 1) * gw] = res.astype(BF16)


def _sb(qb16, kb16, vb16, tri, *, tq, tk, q_off, group=SB_GROUP):
    b, lq, _ = qb16.shape
    lk = kb16.shape[1]
    qblk = lambda bb, i: (bb, i, 0)
    kblk = lambda bb, i: (bb, 0, 0)
    kern = functools.partial(_sb_kernel, tq=tq, tk=tk, q_off=q_off, group=group)
    n_groups, gw = N_HEADS // group, group * HEAD_DIM
    return pl.pallas_call(
        kern,
        grid=(b, lq // tq),
        in_specs=[pl.BlockSpec((None, tq, WIDTH), qblk),
                  pl.BlockSpec((None, lk, WIDTH), kblk),
                  pl.BlockSpec((None, lk, WIDTH), kblk),
                  _resident((tk, tk), lambda bb, i: (0, 0))],
        out_specs=pl.BlockSpec((None, tq, WIDTH), qblk),
        out_shape=jax.ShapeDtypeStruct((b, lq, WIDTH), BF16),
        scratch_shapes=[pltpu.VMEM((n_groups, group * tq, gw), BF16),
                        pltpu.VMEM((n_groups, group * tq, LANES), F32),
                        pltpu.VMEM((n_groups, group * tq, gw), F32)],
        compiler_params=pltpu.CompilerParams(dimension_semantics=("arbitrary", "arbitrary"),
                                             vmem_limit_bytes=VMEM_LIMIT),
        name="sb",
    )(qb16, kb16, vb16, tri)


FF_CHUNK = 1024


def _rms_rows(x, g):
    return x * lax.rsqrt(jnp.mean(x * x, axis=-1, keepdims=True) + EPS) * g


def _merge_kernel(x_ref, oa_ref, ob_ref, gmix_ref, gffn_ref, wg_ref, wba_ref, wbb_ref,
                  wout_ref, wup_ref, wdown_ref, y_ref):
    x = x_ref[...]
    hx = _rms_rows(x, gmix_ref[...]).astype(BF16)
    gate_a = jnp.dot(hx, wg_ref[:, :D_MODEL], preferred_element_type=F32)
    gate_b = jnp.dot(hx, wg_ref[:, D_MODEL:], preferred_element_type=F32)
    pa = jnp.dot(oa_ref[...], wba_ref[...], preferred_element_type=F32)
    pb = jnp.dot(ob_ref[...], wbb_ref[...], preferred_element_type=F32)
    m = jax.nn.sigmoid(gate_a) * pa + jax.nn.sigmoid(gate_b) * pb
    h = x + jnp.dot(m.astype(BF16), wout_ref[...], preferred_element_type=F32)
    hn = _rms_rows(h, gffn_ref[...]).astype(BF16)
    y = h
    for c in range(D_FF // FF_CHUNK):
        u = jnp.dot(hn, wup_ref[:, c * FF_CHUNK:(c + 1) * FF_CHUNK], preferred_element_type=F32)
        r = jnp.square(jnp.maximum(u, 0.0)).astype(BF16)
        y = y + jnp.dot(r, wdown_ref[c * FF_CHUNK:(c + 1) * FF_CHUNK, :],
                        preferred_element_type=F32)
    y_ref[...] = y


def _merge_ffn(x2d, oa16, ob16, wts, tm):
    n = x2d.shape[0]
    row = lambda i: (i, 0)
    const = lambda i: (0, 0)
    weights = [wts["w_gate"], wts["w_branch_a"], wts["w_branch_b"], wts["w_out"],
               wts["w_up"], wts["w_down"]]
    return pl.pallas_call(
        _merge_kernel,
        grid=(n // tm,),
        in_specs=[pl.BlockSpec((tm, D_MODEL), row),
                  pl.BlockSpec((tm, WIDTH), row),
                  pl.BlockSpec((tm, WIDTH), row),
                  pl.BlockSpec((1, D_MODEL), const),
                  pl.BlockSpec((1, D_MODEL), const)]
                 + [_resident(w.shape, const) for w in weights],
        out_specs=pl.BlockSpec((tm, D_MODEL), row),
        out_shape=jax.ShapeDtypeStruct((n, D_MODEL), F32),
        compiler_params=pltpu.CompilerParams(dimension_semantics=("arbitrary",),
                                             vmem_limit_bytes=VMEM_LIMIT),
        name="merge_ffn",
    )(x2d, oa16, ob16, wts["g_mix"], wts["g_ffn"], *weights)


def _rope_tables(pos, rows):
    half = HEAD_DIM // 2
    inv_freq = jnp.power(ROPE_THETA, -jnp.arange(half, dtype=F32) / half)
    ang = pos.astype(F32)[:, None] * inv_freq[None, :]
    cos, sin = jnp.cos(ang), jnp.sin(ang)
    cs = jnp.concatenate([cos, cos, cos, cos], axis=1)
    sn = jnp.concatenate([-sin, sin, -sin, sin], axis=1)
    reps = rows // pos.shape[0]
    if reps > 1:
        cs, sn = jnp.tile(cs, (reps, 1)), jnp.tile(sn, (reps, 1))
    return cs, sn


def _prep_weights(g_mix, w_in, g_qn, g_kn, w_branch_a, w_branch_b, w_out, g_ffn, w_up, w_down):
    offs = [int(v) for v in np.cumsum(SPLIT_SIZES)[:-1]]
    w_qa, w_ka, w_va, w_qi, w_ki, w_wi, w_qb, w_kb, w_vb, w_ga, w_gb = jnp.split(w_in, offs, axis=1)
    pad = jnp.zeros((D_MODEL, LANES - IDX_DIM - N_HEADS), w_in.dtype)
    w_proj = jnp.concatenate([w_qa, w_ka, w_va, w_qi, w_qb, w_kb, w_vb, w_ki, w_wi, pad], axis=1)
    head = np.arange(WIDTH) // HEAD_DIM
    return {
        "w_proj": w_proj.astype(BF16),
        "w_gate": jnp.concatenate([w_ga, w_gb], axis=1).astype(BF16),
        "grp": jnp.asarray(head[:, None] == head[None, :], BF16),
        "g_mix": g_mix.reshape(1, D_MODEL),
        "g_ffn": g_ffn.reshape(1, D_MODEL),
        "g_qn": jnp.tile(g_qn, N_HEADS).reshape(1, WIDTH),
        "g_kn": jnp.tile(g_kn, N_HEADS).reshape(1, WIDTH),
        "q_norm_max": (HEAD_DIM ** 0.5 * jnp.max(jnp.abs(g_qn))).astype(F32),
        "k_norm_max": (HEAD_DIM ** 0.5 * jnp.max(jnp.abs(g_kn))).astype(F32),
        "w_branch_a": w_branch_a.astype(BF16),
        "w_branch_b": w_branch_b.astype(BF16),
        "w_out": w_out.astype(BF16),
        "w_up": w_up.astype(BF16),
        "w_down": w_down.astype(BF16),
    }


def _tri(tk):
    r = np.arange(tk)
    return jnp.asarray(r[:, None] > r[None, :], BF16)


def _pad_keys(a, lk):
    return jnp.pad(a, ((0, 0), (0, lk - a.shape[1]), (0, 0)))


def _layer(x, pos, wts, caches, *, tq_dsa, tk_dsa, tq_sb, tk_sb):
    b, l, _ = x.shape
    n = b * l
    tm = min(512, n)
    x2d = x.reshape(n, D_MODEL)
    rows = l if l >= tm else tm
    (ka, va, ki, kb, vb, qa16, ka16, va16, qi16, ki16, qb16, kb16, vb16, wi) = _project(
        x2d, _rope_tables(pos, rows), wts, tm)
    per_seq = lambda a: a.reshape(b, l, a.shape[-1])
    qa16, qi16, qb16, wi = per_seq(qa16), per_seq(qi16), per_seq(qb16), per_seq(wi)
    ka16, va16, ki16, kb16, vb16 = (per_seq(a) for a in (ka16, va16, ki16, kb16, vb16))
    past = 0 if caches is None else caches[0].shape[1]
    n_valid = past + l
    lk_dsa = pl.cdiv(n_valid, tk_dsa) * tk_dsa
    lk_sb = pl.cdiv(n_valid, tk_sb) * tk_sb

    def keys(new16, cache, lk, dup=1):
        if cache is None:
            return _pad_keys(new16, lk)
        c16 = cache.reshape(b, past, -1).astype(BF16)
        if dup > 1:
            c16 = jnp.concatenate([c16] * dup, axis=2)
        return lax.dynamic_update_slice(_pad_keys(c16, lk), new16, (0, past, 0))

    c_ka, c_va, c_ki, c_kb, c_vb = caches if caches is not None else (None,) * 5
    k_norm_max = wts["k_norm_max"]
    if c_ka is not None:
        seen = c_ka.astype(BF16).astype(F32)
        k_norm_max = jnp.maximum(k_norm_max, jnp.sqrt(jnp.max(jnp.sum(seen * seen, axis=-1))))
    logit_bound = (QK_SCALE * wts["q_norm_max"] * k_norm_max).reshape(1)
    oa16 = _dsa(logit_bound, qa16, qi16, wi, keys(ki16, c_ki, lk_dsa, dup=2),
                keys(ka16, c_ka, lk_dsa), keys(va16, c_va, lk_dsa),
                tq=tq_dsa, tk=tk_dsa, n_valid=n_valid, q_off=past)
    ob16 = _sb(qb16, keys(kb16, c_kb, lk_sb), keys(vb16, c_vb, lk_sb), _tri(tk_sb),
               tq=tq_sb, tk=tk_sb, q_off=past)
    y = _merge_ffn(x2d, oa16.reshape(n, WIDTH), ob16.reshape(n, WIDTH), wts, tm)
    heads = lambda a: a.reshape(b, l, N_HEADS, HEAD_DIM)
    return (y.reshape(b, l, D_MODEL), heads(ka), heads(va), ki.reshape(b, l, IDX_DIM),
            heads(kb), heads(vb))


def kernel(x_prompt, x_sample, cache_k_a, cache_v_a, cache_k_idx, cache_k_sb, cache_v_sb,
           g_mix, w_in, g_qn, g_kn, w_branch_a, w_branch_b, w_out, g_ffn, w_up, w_down):
    wts = _prep_weights(g_mix, w_in, g_qn, g_kn, w_branch_a, w_branch_b, w_out, g_ffn, w_up, w_down)
    s = x_prompt.shape[1]
    t = x_sample.shape[1]
    past = cache_k_a.shape[1]
    prompt = _layer(x_prompt, jnp.arange(s, dtype=jnp.int32), wts, None,
                    tq_dsa=min(256, s), tk_dsa=min(1024, s), tq_sb=min(256, s), tk_sb=min(256, s))
    sample = _layer(x_sample, past + jnp.arange(t, dtype=jnp.int32), wts,
                    (cache_k_a, cache_v_a, cache_k_idx, cache_k_sb, cache_v_sb),
                    tq_dsa=LANES, tk_dsa=512, tq_sb=t, tk_sb=128)
    return (prompt[0], sample[0]) + prompt[1:] + sample[1:]
```
